```python
import jax, jax.numpy as jnp
from jax import lax
import numpy as np

D_MODEL = 1024
BATCH = 2
SEQ = 8192
DEPTH = 4

MIX_WIDTH = D_MODEL
HGRN_WIDTH = MIX_WIDTH // 2
HGRN_HEADS = 4
HGRN_HEAD_DIM = HGRN_WIDTH // HGRN_HEADS
CONV_WIDTH = MIX_WIDTH - HGRN_WIDTH
CONV_GROUPS = 8
CONV_K = 3
FFN_CONV_K = 3
D_FF = 2816
CHUNK = 64
EPS = 1e-6
IN_COLS = 4 * HGRN_WIDTH + 3 * CONV_WIDTH

kernel_name = "hgrn2_shortconv_parallel_hybrid"


def rmsnorm(x, g):
    xf = x.astype(jnp.float32)
    y = xf * lax.rsqrt(jnp.mean(xf * xf, axis=-1, keepdims=True) + EPS)
    return (y * g.astype(jnp.float32)).astype(x.dtype)


def causal_dwconv(x, w):
    k = w.shape[0]
    c = x.shape[-1]
    return lax.conv_general_dilated(
        x, w[:, None, :].astype(x.dtype), window_strides=(1,), padding=[(k - 1, 0)],
        dimension_numbers=("NWC", "WIO", "NWC"), feature_group_count=c)


def to_chunks(t):
    b, t_len, _ = t.shape
    t = t.reshape(b, t_len // CHUNK, CHUNK, HGRN_HEADS, HGRN_HEAD_DIM)
    return t.transpose(1, 0, 3, 2, 4)


def from_chunks(t):
    n, b, h, c, d = t.shape
    return t.transpose(1, 0, 3, 2, 4).reshape(b, n * c, h * d)


def hgrn2_chunkwise(q, k, v, log_f):
    bsz = q.shape[0]
    causal = jnp.tril(jnp.ones((CHUNK, CHUNK), dtype=bool))

    def step(state, inp):
        qc, kc, vc, gc = inp
        b = jnp.cumsum(gc, axis=-2)
        diff = b[..., :, None, :] - b[..., None, :, :]
        decay = jnp.exp(jnp.where(causal[:, :, None], diff, -jnp.inf))
        scores = jnp.einsum('bhtk,bhsk,bhtsk->bhts', qc, kc, decay)
        o = (jnp.einsum('bhts,bhsv->bhtv', scores, vc)
             + jnp.einsum('bhtk,bhkv->bhtv', qc * jnp.exp(b), state))
        b_last = b[..., -1:, :]
        state = (jnp.exp(b_last[..., 0, :])[..., None] * state
                 + jnp.einsum('bhsk,bhsv->bhkv', kc * jnp.exp(b_last - b), vc))
        return state, o

    s0 = jnp.zeros((bsz, HGRN_HEADS, HGRN_HEAD_DIM, HGRN_HEAD_DIM), jnp.float32)
    _, o = lax.scan(step, s0, (to_chunks(q), to_chunks(k), to_chunks(v), to_chunks(log_f)))
    return from_chunks(o)


def hgrn2_mixer(q_pre, f_pre, i_pre, g_pre, lb, norm_g):
    dt = q_pre.dtype
    fp = f_pre.astype(jnp.float32)
    lbf = lb.astype(jnp.float32)
    q = jax.nn.silu(q_pre.astype(jnp.float32)) * (HGRN_HEAD_DIM ** -0.5)
    log_f = jnp.logaddexp(jnp.log(lbf), jnp.log1p(-lbf) + jax.nn.log_sigmoid(fp))
    k = (1.0 - lbf) * jax.nn.sigmoid(-fp)
    v = i_pre.astype(jnp.float32)
    o = hgrn2_chunkwise(q, k, v, log_f)
    bsz, t_len, _ = o.shape
    oh = o.reshape(bsz, t_len, HGRN_HEADS, HGRN_HEAD_DIM)
    oh = oh * lax.rsqrt(jnp.mean(oh * oh, axis=-1, keepdims=True) + EPS)
    o = oh.reshape(bsz, t_len, HGRN_WIDTH) * norm_g.astype(jnp.float32)
    return (o * jax.nn.silu(g_pre.astype(jnp.float32))).astype(dt)


def setup_inputs(seed: int = 0) -> dict:
    key = jax.random.key(seed)
    ks = jax.random.split(key, 14)
    f32 = jnp.float32

    def gain(k, shape):
        return 1.0 + 0.05 * jax.random.normal(k, shape, f32)

    return {
        "x": jax.random.normal(ks[0], (BATCH, SEQ, D_MODEL), f32),
        "lb_param": jax.random.normal(ks[1], (DEPTH, HGRN_WIDTH), f32),
        "w_in": jax.random.normal(ks[2], (DEPTH, D_MODEL, IN_COLS), f32) * D_MODEL ** -0.5,
        "w_out": jax.random.normal(ks[3], (DEPTH, MIX_WIDTH, D_MODEL), f32) * MIX_WIDTH ** -0.5,
        "conv_w": jax.random.normal(ks[4], (DEPTH, CONV_K, CONV_WIDTH), f32) * CONV_K ** -0.5,
        "ffn_w_up": jax.random.normal(ks[5], (DEPTH, D_MODEL, 2 * D_FF), f32) * D_MODEL ** -0.5,
        "ffn_conv_w": jax.random.normal(ks[6], (DEPTH, FFN_CONV_K, 2 * D_FF), f32) * FFN_CONV_K ** -0.5,
        "ffn_w_down": jax.random.normal(ks[7], (DEPTH, D_FF, D_MODEL), f32) * D_FF ** -0.5,
        "hgrn_norm_g": gain(ks[8], (DEPTH, HGRN_WIDTH)),
        "pre_mix_g": gain(ks[9], (DEPTH, D_MODEL)),
        "post_mix_g": gain(ks[10], (DEPTH, D_MODEL)),
        "pre_ffn_g": gain(ks[11], (DEPTH, D_MODEL)),
        "post_ffn_g": gain(ks[12], (DEPTH, D_MODEL)),
    }


def reference(x, lb_param, w_in, w_out, conv_w, ffn_w_up, ffn_conv_w, ffn_w_down,
              hgrn_norm_g, pre_mix_g, post_mix_g, pre_ffn_g, post_ffn_g):
    lb_all = jnp.cumsum(jax.nn.softmax(lb_param.astype(jnp.float32), axis=0), axis=0)
    lb_all = lb_all - lb_all[0:1]
    split_pts = [HGRN_WIDTH * j for j in range(1, 5)] + [4 * HGRN_WIDTH + CONV_WIDTH * j for j in range(1, 3)]

    for l in range(DEPTH):
        h = rmsnorm(x, pre_mix_g[l])
        z = jnp.einsum('btd,dc->btc', h, w_in[l])
        q_p, f_p, i_p, g_p, gate_b, gate_c, conv_h = jnp.split(z, split_pts, axis=-1)
        mix_a = hgrn2_mixer(q_p, f_p, i_p, g_p, lb_all[l], hgrn_norm_g[l])
        mix_b = gate_b * causal_dwconv(gate_c * conv_h, conv_w[l])
        mix = jnp.einsum('btc,cd->btd', jnp.concatenate([mix_a, mix_b], axis=-1), w_out[l])
        x = x + rmsnorm(mix, post_mix_g[l])
        h = rmsnorm(x, pre_ffn_g[l])
        u = causal_dwconv(jnp.einsum('btd,df->btf', h, ffn_w_up[l]), ffn_conv_w[l])
        gate, val = jnp.split(u, 2, axis=-1)
        y = jnp.einsum('btf,fd->btd', jax.nn.silu(gate) * val, ffn_w_down[l])
        x = x + rmsnorm(y, post_ffn_g[l])
    return x
```

```python
import functools

import jax
import jax.numpy as jnp
from jax import lax
from jax.experimental import pallas as pl
from jax.experimental.pallas import tpu as pltpu

D_MODEL = 1024
DEPTH = 4
HGRN_WIDTH = 512
HEADS = 4
HEAD_DIM = 128
CONV_WIDTH = 512
IN_COLS = 4 * HGRN_WIDTH + 3 * CONV_WIDTH
D_FF = 2816
EPS = 1e-6

TQ = 256
CHUNK = 64
SUB = 16
NSUB = CHUNK // SUB
HALO = 8
NEG_BIG = -1e30
VMEM_LIMIT_BYTES = 56 * 1024 * 1024

F32 = jnp.float32
BF16 = jnp.bfloat16


def _dot(a, b):
    return jnp.dot(a, b, preferred_element_type=F32)


def _dot_nt(a, b):
    return lax.dot_general(a, b, (((1,), (1,)), ((), ())), preferred_element_type=F32)


def _dot_tn(a, b):
    return lax.dot_general(a, b, (((0,), (0,)), ((), ())), preferred_element_type=F32)


def _rmsnorm(x, g):
    ms = jnp.mean(x * x, axis=-1, keepdims=True)
    return x * lax.rsqrt(ms + EPS) * g


def _sigmoid(x):
    return 1.0 / (1.0 + jnp.exp(-x))


def _layer_lower_bound(lb_ref, layer):
    lbp = lb_ref[...]
    m = jnp.max(lbp, axis=0, keepdims=True)
    e = jnp.exp(lbp - m)
    p = e / jnp.sum(e, axis=0, keepdims=True)
    lb = p[1:2]
    for j in range(2, layer + 1):
        lb = lb + p[j:j + 1]
    return lb


def _bcast_row(ref, head, row, nrows):
    return jnp.broadcast_to(ref[head, pl.ds(row, 1), :], (nrows, HEAD_DIM))


def _hgrn_head_offdiag(head, q, k, v, b, st_ref, b_s):
    blk = [slice(SUB * i, SUB * (i + 1)) for i in range(NSUB)]
    bend = [_bcast_row(b_s, head, SUB * i + SUB - 1, SUB) for i in range(NSUB)]
    qp = [q[blk[0]] * jnp.exp(b[blk[0]])]
    for i in range(1, NSUB):
        qp.append(q[blk[i]] * jnp.exp(b[blk[i]] - bend[i - 1]))
    kp = [k[blk[i]] * jnp.exp(bend[i] - b[blk[i]]) for i in range(NSUB)]

    st = st_ref[head]
    st_b = st.astype(BF16)
    qb = [qp[0]] + [qp[i] * jnp.exp(bend[i - 1]) for i in range(1, NSUB)]
    o_inter = _dot_nt(jnp.concatenate(qb, axis=0).astype(BF16), st_b)
    out = [o_inter[blk[i]] for i in range(NSUB)]

    v_b = v.astype(BF16)
    for i in range(1, NSUB):
        parts = [kp[j] * jnp.exp(bend[i - 1] - bend[j]) for j in range(i - 1)] + [kp[i - 1]]
        kcat = jnp.concatenate(parts, axis=0) if len(parts) > 1 else parts[0]
        a = _dot_nt(qp[i].astype(BF16), kcat.astype(BF16))
        out[i] = out[i] + _dot(a.astype(BF16), v_b[0:SUB * i])

    last = NSUB - 1
    kend = jnp.concatenate(
        [kp[j] * jnp.exp(bend[last] - bend[j]) for j in range(last)] + [kp[last]], axis=0)
    decay = jnp.exp(_bcast_row(b_s, head, CHUNK - 1, HEAD_DIM))
    st_ref[head] = st * decay + _dot_tn(v_b, kend.astype(BF16))
    return out


def _hgrn_pair_diag(heads, qs, bs, out, b_s, k_s, v_s, ones_bd):
    row = lax.broadcasted_iota(jnp.int32, (8, HEAD_DIM), 0)
    for i in range(NSUB):
        pieces = {h: [] for h in heads}
        for h, q, b in zip(heads, qs, bs):
            q_lo, q_hi = q[SUB * i:SUB * i + 8], q[SUB * i + 8:SUB * (i + 1)]
            b_lo, b_hi = b[SUB * i:SUB * i + 8], b[SUB * i + 8:SUB * (i + 1)]
            for s in range(SUB):
                r = SUB * i + s
                b_row = _bcast_row(b_s, h, r, 8)
                k_row = _bcast_row(k_s, h, r, 8)
                if s < 8:
                    d_lo = jnp.where(row >= s, b_lo - b_row, NEG_BIG)
                    pieces[h].append(q_lo * jnp.exp(d_lo) * k_row)
                    pieces[h].append(q_hi * jnp.exp(b_hi - b_row) * k_row)
                else:
                    d_hi = jnp.where(row >= s - 8, b_hi - b_row, NEG_BIG)
                    pieces[h].append(q_hi * jnp.exp(d_hi) * k_row)
        p = jnp.concatenate(
            [jnp.concatenate(pieces[h], axis=0) for h in heads], axis=1).astype(BF16)
        red = _dot(p, ones_bd)
        for hi, h in enumerate(heads):
            lanes = slice(HEAD_DIM * hi, HEAD_DIM * (hi + 1))
            acc_lo = out[h][i][0:8]
            acc_hi = out[h][i][8:SUB]
            n = 0
            for s in range(SUB):
                v_row = _bcast_row(v_s, h, SUB * i + s, 8)
                if s < 8:
                    acc_lo = acc_lo + red[8 * n:8 * n + 8, lanes] * v_row
                    n += 1
                acc_hi = acc_hi + red[8 * n:8 * n + 8, lanes] * v_row
                n += 1
            out[h][i] = jnp.concatenate([acc_lo, acc_hi], axis=0)


def _mixer_kernel(layer, x_ref, lb_ref, w_in_ref, w_out_ref, cw_ref, hg_ref, pre_g_ref,
                  post_g_ref, o_ref, z_s, mix_s, cbuf, st_s, b_s, k_s, v_s):
    t = pl.program_id(1)

    @pl.when(t == 0)
    def _():
        st_s[...] = jnp.zeros_like(st_s)
        cbuf[pl.ds(TQ, HALO), :] = jnp.zeros((HALO, CONV_WIDTH), F32)

    x = x_ref[...]
    h = _rmsnorm(x, pre_g_ref[...]).astype(BF16)
    for g in range(IN_COLS // 512):
        cols = slice(512 * g, 512 * (g + 1))
        z_s[:, cols] = _dot(h, w_in_ref[:, cols])

    if layer > 0:
        lb = _layer_lower_bound(lb_ref, layer)
        log_lb = jnp.log(lb)
        log_1m = jnp.log1p(-lb)
        one_m = 1.0 - lb

    tri = jnp.where(
        lax.broadcasted_iota(jnp.int32, (CHUNK, CHUNK), 0)
        >= lax.broadcasted_iota(jnp.int32, (CHUNK, CHUNK), 1), 1.0, 0.0).astype(BF16)
    ones_bd = jnp.where(
        lax.broadcasted_iota(jnp.int32, (2 * HEAD_DIM, 2 * HEAD_DIM), 0) // HEAD_DIM
        == lax.broadcasted_iota(jnp.int32, (2 * HEAD_DIM, 2 * HEAD_DIM), 1) // HEAD_DIM,
        1.0, 0.0).astype(BF16)
    hg = hg_ref[...]

    def chunk_body(ci, carry):
        rows = pl.ds(pl.multiple_of(ci * CHUNK, CHUNK), CHUNK)
        fp = z_s[rows, 512:1024]
        e = jnp.exp(-jnp.abs(fp))
        log_sig = jnp.minimum(fp, 0.0) - jnp.log1p(e)
        r = 1.0 / (1.0 + e)
        sig_neg = jnp.where(fp >= 0.0, e * r, r)
        if layer == 0:
            log_f, kk = log_sig, sig_neg
        else:
            cc = log_1m + log_sig
            log_f = jnp.maximum(log_lb, cc) + jnp.log1p(jnp.exp(-jnp.abs(log_lb - cc)))
            kk = one_m * sig_neg
        g1 = log_f.astype(BF16)
        r1 = log_f - g1.astype(F32)
        g2 = r1.astype(BF16)
        g3 = (r1 - g2.astype(F32)).astype(BF16)
        b = _dot(tri, g1) + _dot(tri, g2) + _dot(tri, g3)

        qz = z_s[rows, 0:512]
        q = qz * _sigmoid(qz) * (HEAD_DIM ** -0.5)
        v = z_s[rows, 1024:1536]
        gz = z_s[rows, 1536:2048]
        gate = gz * _sigmoid(gz)

        b_s[...] = jnp.stack([b[:, HEAD_DIM * hd:HEAD_DIM * (hd + 1)] for hd in range(HEADS)])
        k_s[...] = jnp.stack([kk[:, HEAD_DIM * hd:HEAD_DIM * (hd + 1)] for hd in range(HEADS)])
        v_s[...] = jnp.stack([v[:, HEAD_DIM * hd:HEAD_DIM * (hd + 1)] for hd in range(HEADS)])

        out = {}
        for hd in range(HEADS):
            lanes = slice(HEAD_DIM * hd, HEAD_DIM * (hd + 1))
            out[hd] = _hgrn_head_offdiag(hd, q[:, lanes], kk[:, lanes], v[:, lanes], b[:, lanes],
                                         st_s, b_s)
        for pair in ((0, 1), (2, 3)):
            _hgrn_pair_diag(pair,
                            [q[:, HEAD_DIM * hd:HEAD_DIM * (hd + 1)] for hd in pair],
                            [b[:, HEAD_DIM * hd:HEAD_DIM * (hd + 1)] for hd in pair],
                            out, b_s, k_s, v_s, ones_bd)
        for hd in range(HEADS):
            lanes = slice(HEAD_DIM * hd, HEAD_DIM * (hd + 1))
            oh = jnp.concatenate(out[hd], axis=0)
            ms = jnp.mean(oh * oh, axis=-1, keepdims=True)
            on = oh * lax.rsqrt(ms + EPS) * hg[:, lanes] * gate[:, lanes]
            mix_s[rows, lanes] = on.astype(BF16)
        return carry

    lax.fori_loop(0, TQ // CHUNK, chunk_body, 0)

    cbuf[0:HALO, :] = cbuf[TQ:TQ + HALO, :]
    cbuf[HALO:HALO + TQ, :] = z_s[:, 2560:3072] * z_s[:, 3072:3584]
    cw = cw_ref[...]
    y = (cw[2:3] * cbuf[HALO:HALO + TQ, :] + cw[1:2] * cbuf[HALO - 1:HALO - 1 + TQ, :]
         + cw[0:1] * cbuf[HALO - 2:HALO - 2 + TQ, :])
    mix_s[:, HGRN_WIDTH:] = (z_s[:, 2048:2560] * y).astype(BF16)

    mix = _dot(mix_s[...], w_out_ref[...])
    o_ref[...] = x + _rmsnorm(mix, post_g_ref[...])


def _ffn_kernel(x_ref, w_up_ref, cw_ref, w_down_ref, pre_g_ref, post_g_ref, o_ref, ubuf, act_s):
    t = pl.program_id(1)

    @pl.when(t == 0)
    def _():
        ubuf[pl.ds(TQ, HALO), :] = jnp.zeros((HALO, 2 * D_FF), F32)

    ubuf[0:HALO, :] = ubuf[TQ:TQ + HALO, :]
    x = x_ref[...]
    h = _rmsnorm(x, pre_g_ref[...]).astype(BF16)
    for g in range(2 * D_FF // 512):
        cols = slice(512 * g, 512 * (g + 1))
        ubuf[HALO:HALO + TQ, cols] = _dot(h, w_up_ref[:, cols])

    def conv(cols):
        return (cw_ref[2:3, cols] * ubuf[HALO:HALO + TQ, cols]
                + cw_ref[1:2, cols] * ubuf[HALO - 1:HALO - 1 + TQ, cols]
                + cw_ref[0:1, cols] * ubuf[HALO - 2:HALO - 2 + TQ, cols])

    for g in range(D_FF // 256):
        gate = conv(slice(256 * g, 256 * (g + 1)))
        val = conv(slice(D_FF + 256 * g, D_FF + 256 * (g + 1)))
        act_s[:, 256 * g:256 * (g + 1)] = (gate * _sigmoid(gate) * val).astype(BF16)

    y = _dot(act_s[...], w_down_ref[...])
    o_ref[...] = x + _rmsnorm(y, post_g_ref[...])


def _x_spec():
    return pl.BlockSpec((None, TQ, D_MODEL), lambda b, t: (b, t, 0))


def _layer_spec(shape, layer):
    return pl.BlockSpec((None,) + shape, lambda b, t: (layer,) + (0,) * len(shape))


_COMPILER_PARAMS = pltpu.CompilerParams(
    dimension_semantics=("arbitrary", "arbitrary"), vmem_limit_bytes=VMEM_LIMIT_BYTES)


def _mixer_call(layer, x, lb_param, w_in, w_out, conv_w, hgrn_g, pre_g, post_g):
    batch, seq, _ = x.shape
    return pl.pallas_call(
        functools.partial(_mixer_kernel, layer),
        out_shape=jax.ShapeDtypeStruct(x.shape, x.dtype),
        grid=(batch, seq // TQ),
        in_specs=[
            _x_spec(),
            pl.BlockSpec((DEPTH, HGRN_WIDTH), lambda b, t: (0, 0)),
            _layer_spec((D_MODEL, IN_COLS), layer),
            _layer_spec((D_MODEL, D_MODEL), layer),
            _layer_spec((3, CONV_WIDTH), layer),
            _layer_spec((1, HGRN_WIDTH), layer),
            _layer_spec((1, D_MODEL), layer),
            _layer_spec((1, D_MODEL), layer),
        ],
        out_specs=_x_spec(),
        scratch_shapes=[
            pltpu.VMEM((TQ, IN_COLS), F32),
            pltpu.VMEM((TQ, D_MODEL), BF16),
            pltpu.VMEM((TQ + HALO, CONV_WIDTH), F32),
            pltpu.VMEM((HEADS, HEAD_DIM, HEAD_DIM), F32),
            pltpu.VMEM((HEADS, CHUNK, HEAD_DIM), F32),
            pltpu.VMEM((HEADS, CHUNK, HEAD_DIM), F32),
            pltpu.VMEM((HEADS, CHUNK, HEAD_DIM), F32),
        ],
        compiler_params=_COMPILER_PARAMS,
        name=f"mixer_l{layer}",
    )(x, lb_param, w_in, w_out, conv_w, hgrn_g, pre_g, post_g)


def _ffn_call(layer, x, w_up, conv_w, w_down, pre_g, post_g):
    batch, seq, _ = x.shape
    return pl.pallas_call(
        _ffn_kernel,
        out_shape=jax.ShapeDtypeStruct(x.shape, x.dtype),
        grid=(batch, seq // TQ),
        in_specs=[
            _x_spec(),
            _layer_spec((D_MODEL, 2 * D_FF), layer),
            _layer_spec((3, 2 * D_FF), layer),
            _layer_spec((D_FF, D_MODEL), layer),
            _layer_spec((1, D_MODEL), layer),
            _layer_spec((1, D_MODEL), layer),
        ],
        out_specs=_x_spec(),
        scratch_shapes=[
            pltpu.VMEM((TQ + HALO, 2 * D_FF), F32),
            pltpu.VMEM((TQ, D_FF), BF16),
        ],
        compiler_params=_COMPILER_PARAMS,
        name=f"ffn_l{layer}",
    )(x, w_up, conv_w, w_down, pre_g, post_g)


def kernel(x, lb_param, w_in, w_out, conv_w, ffn_w_up, ffn_conv_w, ffn_w_down, hgrn_norm_g,
           pre_mix_g, post_mix_g, pre_ffn_g, post_ffn_g):
    assert x.shape[1] % TQ == 0 and x.shape[2] == D_MODEL
    w_in_b = w_in.astype(BF16)
    w_out_b = w_out.astype(BF16)
    w_up_b = ffn_w_up.astype(BF16)
    w_down_b = ffn_w_down.astype(BF16)
    hgrn_g = hgrn_norm_g[:, None, :]
    pre_mix, post_mix = pre_mix_g[:, None, :], post_mix_g[:, None, :]
    pre_ffn, post_ffn = pre_ffn_g[:, None, :], post_ffn_g[:, None, :]
    for layer in range(DEPTH):
        x = _mixer_call(layer, x, lb_param, w_in_b, w_out_b, conv_w, hgrn_g, pre_mix, post_mix)
        x = _ffn_call(layer, x, w_up_b, ffn_conv_w, w_down_b, pre_ffn, post_ffn)
    return x
```

```python
import functools

import jax
import jax.numpy as jnp
from jax import lax
from jax.experimental import pallas as pl
from jax.experimental.pallas import tpu as pltpu

D_MODEL = 1024
DEPTH = 4
HGRN_WIDTH = 512
HEADS = 4
HEAD_DIM = 128
CONV_WIDTH = 512
IN_COLS = 4 * HGRN_WIDTH + 3 * CONV_WIDTH
D_FF = 2816
EPS = 1e-6

TQ = 256
FCHUNK = 128
MAX_LOG_SPAN = 80.0
CHUNK = 64
SUB = 16
NSUB = CHUNK // SUB
HALO = 8
NEG_BIG = -1e30
VMEM_LIMIT_BYTES = 56 * 1024 * 1024

F32 = jnp.float32
BF16 = jnp.bfloat16


def _dot(a, b):
    return jnp.dot(a, b, preferred_element_type=F32)


def _dot_nt(a, b):
    return lax.dot_general(a, b, (((1,), (1,)), ((), ())), preferred_element_type=F32)


def _dot_tn(a, b):
    return lax.dot_general(a, b, (((0,), (0,)), ((), ())), preferred_element_type=F32)


def _rmsnorm(x, g):
    ms = jnp.mean(x * x, axis=-1, keepdims=True)
    return x * lax.rsqrt(ms + EPS) * g


def _sigmoid(x):
    return 1.0 / (1.0 + jnp.exp(-x))


def _layer_lower_bound(lb_ref, layer):
    lbp = lb_ref[...]
    m = jnp.max(lbp, axis=0, keepdims=True)
    e = jnp.exp(lbp - m)
    p = e / jnp.sum(e, axis=0, keepdims=True)
    lb = p[1:2]
    for j in range(2, layer + 1):
        lb = lb + p[j:j + 1]
    return lb


def _bcast_row(ref, head, row, nrows):
    return jnp.broadcast_to(ref[head, pl.ds(row, 1), :], (nrows, HEAD_DIM))


def _hgrn_head_offdiag(head, q, k, v, b, st_ref, b_s):
    blk = [slice(SUB * i, SUB * (i + 1)) for i in range(NSUB)]
    bend = [_bcast_row(b_s, head, SUB * i + SUB - 1, SUB) for i in range(NSUB)]
    qp = [q[blk[0]] * jnp.exp(b[blk[0]])]
    for i in range(1, NSUB):
        qp.append(q[blk[i]] * jnp.exp(b[blk[i]] - bend[i - 1]))
    kp = [k[blk[i]] * jnp.exp(bend[i] - b[blk[i]]) for i in range(NSUB)]

    st = st_ref[head]
    st_b = st.astype(BF16)
    qb = [qp[0]] + [qp[i] * jnp.exp(bend[i - 1]) for i in range(1, NSUB)]
    o_inter = _dot_nt(jnp.concatenate(qb, axis=0).astype(BF16), st_b)
    out = [o_inter[blk[i]] for i in range(NSUB)]

    v_b = v.astype(BF16)
    for i in range(1, NSUB):
        parts = [kp[j] * jnp.exp(bend[i - 1] - bend[j]) for j in range(i - 1)] + [kp[i - 1]]
        kcat = jnp.concatenate(parts, axis=0) if len(parts) > 1 else parts[0]
        a = _dot_nt(qp[i].astype(BF16), kcat.astype(BF16))
        out[i] = out[i] + _dot(a.astype(BF16), v_b[0:SUB * i])

    last = NSUB - 1
    kend = jnp.concatenate(
        [kp[j] * jnp.exp(bend[last] - bend[j]) for j in range(last)] + [kp[last]], axis=0)
    decay = jnp.exp(_bcast_row(b_s, head, CHUNK - 1, HEAD_DIM))
    st_ref[head] = st * decay + _dot_tn(v_b, kend.astype(BF16))
    return out


def _hgrn_pair_diag(heads, qs, bs, out, b_s, k_s, v_s, ones_bd):
    row = lax.broadcasted_iota(jnp.int32, (8, HEAD_DIM), 0)
    for i in range(NSUB):
        pieces = {h: [] for h in heads}
        for h, q, b in zip(heads, qs, bs):
            q_lo, q_hi = q[SUB * i:SUB * i + 8], q[SUB * i + 8:SUB * (i + 1)]
            b_lo, b_hi = b[SUB * i:SUB * i + 8], b[SUB * i + 8:SUB * (i + 1)]
            for s in range(SUB):
                r = SUB * i + s
                b_row = _bcast_row(b_s, h, r, 8)
                k_row = _bcast_row(k_s, h, r, 8)
                if s < 8:
                    d_lo = jnp.where(row >= s, b_lo - b_row, NEG_BIG)
                    pieces[h].append(q_lo * jnp.exp(d_lo) * k_row)
                    pieces[h].append(q_hi * jnp.exp(b_hi - b_row) * k_row)
                else:
                    d_hi = jnp.where(row >= s - 8, b_hi - b_row, NEG_BIG)
                    pieces[h].append(q_hi * jnp.exp(d_hi) * k_row)
        p = jnp.concatenate(
            [jnp.concatenate(pieces[h], axis=0) for h in heads], axis=1).astype(BF16)
        red = _dot(p, ones_bd)
        for hi, h in enumerate(heads):
            lanes = slice(HEAD_DIM * hi, HEAD_DIM * (hi + 1))
            acc_lo = out[h][i][0:8]
            acc_hi = out[h][i][8:SUB]
            n = 0
            for s in range(SUB):
                v_row = _bcast_row(v_s, h, SUB * i + s, 8)
                if s < 8:
                    acc_lo = acc_lo + red[8 * n:8 * n + 8, lanes] * v_row
                    n += 1
                acc_hi = acc_hi + red[8 * n:8 * n + 8, lanes] * v_row
                n += 1
            out[h][i] = jnp.concatenate([acc_lo, acc_hi], axis=0)


def _gate_prep(layer, lb_ref, z_s, q_s, k_s, g_s, b_s):
    if layer > 0:
        lb = _layer_lower_bound(lb_ref, layer)
        log_lb = jnp.log(lb)
        log_1m = jnp.log1p(-lb)
        one_m = 1.0 - lb
    for hd in range(HEADS):
        lanes = slice(HEAD_DIM * hd, HEAD_DIM * (hd + 1))
        fp = z_s[:, HGRN_WIDTH + HEAD_DIM * hd:HGRN_WIDTH + HEAD_DIM * (hd + 1)]
        e = jnp.exp(-jnp.abs(fp))
        log_sig = jnp.minimum(fp, 0.0) - jnp.log(1.0 + e)
        r = 1.0 / (1.0 + e)
        sig_neg = jnp.where(fp >= 0.0, e * r, r)
        if layer == 0:
            log_f, kk = log_sig, sig_neg
        else:
            cc = log_1m[:, lanes] + log_sig
            la = log_lb[:, lanes]
            log_f = jnp.maximum(la, cc) + jnp.log(1.0 + jnp.exp(-jnp.abs(la - cc)))
            kk = one_m[:, lanes] * sig_neg
        k_s[:, lanes] = kk
        b_s[:, lanes] = log_f
        qz = z_s[:, lanes]
        q_s[:, lanes] = qz * _sigmoid(qz) * (HEAD_DIM ** -0.5)
        gz = z_s[:, 3 * HGRN_WIDTH + HEAD_DIM * hd:3 * HGRN_WIDTH + HEAD_DIM * (hd + 1)]
        g_s[:, lanes] = gz * _sigmoid(gz)

    tri = jnp.where(
        lax.broadcasted_iota(jnp.int32, (FCHUNK, FCHUNK), 0)
        >= lax.broadcasted_iota(jnp.int32, (FCHUNK, FCHUNK), 1), 1.0, 0.0).astype(BF16)
    span = None
    for c in range(TQ // FCHUNK):
        rows = slice(FCHUNK * c, FCHUNK * (c + 1))
        log_f = b_s[rows, :]
        g1 = log_f.astype(BF16)
        r1 = log_f - g1.astype(F32)
        g2 = r1.astype(BF16)
        g3 = (r1 - g2.astype(F32)).astype(BF16)
        b = _dot(tri, g1) + _dot(tri, g2) + _dot(tri, g3)
        b_s[rows, :] = b
        mid = b[FCHUNK // 2 - 1:FCHUNK // 2]
        s = jnp.maximum(b[0:1] - mid, mid - b[FCHUNK - 1:FCHUNK])
        span = s if span is None else jnp.maximum(span, s)
    return jnp.max(span)


def _hgrn_in_range(z_s, q_s, k_s, g_s, b_s, hg, st_s, mix_s):
    tril = (lax.broadcasted_iota(jnp.int32, (FCHUNK, FCHUNK), 0)
            >= lax.broadcasted_iota(jnp.int32, (FCHUNK, FCHUNK), 1))
    for hd in range(HEADS):
        lanes = slice(HEAD_DIM * hd, HEAD_DIM * (hd + 1))
        st = st_s[hd]
        for c in range(TQ // FCHUNK):
            rows = slice(FCHUNK * c, FCHUNK * (c + 1))
            b = b_s[rows, lanes]
            mid = b_s[FCHUNK * c + FCHUNK // 2 - 1:FCHUNK * c + FCHUNK // 2, lanes]
            last = b_s[FCHUNK * (c + 1) - 1:FCHUNK * (c + 1), lanes]
            v = z_s[rows, 2 * HGRN_WIDTH + HEAD_DIM * hd:2 * HGRN_WIDTH + HEAD_DIM * (hd + 1)]
            v = v.astype(BF16)
            qs = q_s[rows, lanes] * jnp.exp(b - mid)
            ks = k_s[rows, lanes] * jnp.exp(mid - b)
            a = _dot_nt(qs.astype(BF16), ks.astype(BF16))
            a = jnp.where(tril, a, 0.0).astype(BF16)
            qe = (qs * jnp.exp(mid)).astype(BF16)
            o = _dot(a, v) + _dot_nt(qe, st.astype(BF16))
            ke = (ks * jnp.exp(last - mid)).astype(BF16)
            st = st * jnp.exp(last) + _dot_tn(v, ke)
            ms = jnp.mean(o * o, axis=-1, keepdims=True)
            on = o * lax.rsqrt(ms + EPS) * hg[:, lanes] * g_s[rows, lanes]
            mix_s[rows, lanes] = on.astype(BF16)
        st_s[hd] = st


def _hgrn_any_range(z_s, q_s, k_s, g_s, b_s, hg, st_s, mix_s, cb_s, ck_s, cv_s):
    ones_bd = jnp.where(
        lax.broadcasted_iota(jnp.int32, (2 * HEAD_DIM, 2 * HEAD_DIM), 0) // HEAD_DIM
        == lax.broadcasted_iota(jnp.int32, (2 * HEAD_DIM, 2 * HEAD_DIM), 1) // HEAD_DIM,
        1.0, 0.0).astype(BF16)

    def chunk_body(c, carry):
        for half in range(FCHUNK // CHUNK):
            r0 = pl.multiple_of(c * FCHUNK + CHUNK * half, CHUNK)
            rows = pl.ds(r0, CHUNK)
            b = b_s[rows, :]
            if half:
                b = b - b_s[pl.ds(r0 - 1, 1), :]
            q = q_s[rows, :]
            kk = k_s[rows, :]
            v = z_s[rows, 2 * HGRN_WIDTH:3 * HGRN_WIDTH]
            gate = g_s[rows, :]
            cb_s[...] = jnp.stack([b[:, HEAD_DIM * hd:HEAD_DIM * (hd + 1)] for hd in range(HEADS)])
            ck_s[...] = jnp.stack([kk[:, HEAD_DIM * hd:HEAD_DIM * (hd + 1)] for hd in range(HEADS)])
            cv_s[...] = jnp.stack([v[:, HEAD_DIM * hd:HEAD_DIM * (hd + 1)] for hd in range(HEADS)])
            out = {}
            for hd in range(HEADS):
                lanes = slice(HEAD_DIM * hd, HEAD_DIM * (hd + 1))
                out[hd] = _hgrn_head_offdiag(hd, q[:, lanes], kk[:, lanes], v[:, lanes],
                                             b[:, lanes], st_s, cb_s)
            for pair in ((0, 1), (2, 3)):
                _hgrn_pair_diag(pair,
                                [q[:, HEAD_DIM * hd:HEAD_DIM * (hd + 1)] for hd in pair],
                                [b[:, HEAD_DIM * hd:HEAD_DIM * (hd + 1)] for hd in pair],
                                out, cb_s, ck_s, cv_s, ones_bd)
            for hd in range(HEADS):
                lanes = slice(HEAD_DIM * hd, HEAD_DIM * (hd + 1))
                oh = jnp.concatenate(out[hd], axis=0)
                ms = jnp.mean(oh * oh, axis=-1, keepdims=True)
                on = oh * lax.rsqrt(ms + EPS) * hg[:, lanes] * gate[:, lanes]
                mix_s[rows, lanes] = on.astype(BF16)
        return carry

    lax.fori_loop(0, TQ // FCHUNK, chunk_body, 0)


def _mixer_kernel(layer, x_ref, lb_ref, w_in_ref, w_out_ref, cw_ref, hg_ref, pre_g_ref,
                  post_g_ref, o_ref, z_s, q_s, k_s, g_s, b_s, mix_s, cbuf, st_s, cb_s, ck_s, cv_s):
    t = pl.program_id(1)

    @pl.when(t == 0)
    def _():
        st_s[...] = jnp.zeros_like(st_s)
        cbuf[pl.ds(TQ, HALO), :] = jnp.zeros((HALO, CONV_WIDTH), F32)

    x = x_ref[...]
    h = _rmsnorm(x, pre_g_ref[...]).astype(BF16)
    for g in range(IN_COLS // 512):
        cols = slice(512 * g, 512 * (g + 1))
        z_s[:, cols] = _dot(h, w_in_ref[:, cols])

    span = _gate_prep(layer, lb_ref, z_s, q_s, k_s, g_s, b_s)
    in_range = span <= MAX_LOG_SPAN
    hg = hg_ref[...]

    @pl.when(in_range)
    def _():
        _hgrn_in_range(z_s, q_s, k_s, g_s, b_s, hg, st_s, mix_s)

    @pl.when(jnp.logical_not(in_range))
    def _():
        _hgrn_any_range(z_s, q_s, k_s, g_s, b_s, hg, st_s, mix_s, cb_s, ck_s, cv_s)

    cbuf[0:HALO, :] = cbuf[TQ:TQ + HALO, :]
    cbuf[HALO:HALO + TQ, :] = z_s[:, 2560:3072] * z_s[:, 3072:3584]
    cw = cw_ref[...]
    y = (cw[2:3] * cbuf[HALO:HALO + TQ, :] + cw[1:2] * cbuf[HALO - 1:HALO - 1 + TQ, :]
         + cw[0:1] * cbuf[HALO - 2:HALO - 2 + TQ, :])
    mix_s[:, HGRN_WIDTH:] = (z_s[:, 2048:2560] * y).astype(BF16)

    mix = _dot(mix_s[...], w_out_ref[...])
    o_ref[...] = x + _rmsnorm(mix, post_g_ref[...])


def _ffn_kernel(x_ref, w_up_ref, cw_ref, w_down_ref, pre_g_ref, post_g_ref, o_ref, ubuf, act_s):
    t = pl.program_id(1)

    @pl.when(t == 0)
    def _():
        ubuf[pl.ds(TQ, HALO), :] = jnp.zeros((HALO, 2 * D_FF), F32)

    ubuf[0:HALO, :] = ubuf[TQ:TQ + HALO, :]
    x = x_ref[...]
    h = _rmsnorm(x, pre_g_ref[...]).astype(BF16)
    for g in range(2 * D_FF // 512):
        cols = slice(512 * g, 512 * (g + 1))
        ubuf[HALO:HALO + TQ, cols] = _dot(h, w_up_ref[:, cols])

    def conv(cols):
        return (cw_ref[2:3, cols] * ubuf[HALO:HALO + TQ, cols]
                + cw_ref[1:2, cols] * ubuf[HALO - 1:HALO - 1 + TQ, cols]
                + cw_ref[0:1, cols] * ubuf[HALO - 2:HALO - 2 + TQ, cols])

    for g in range(D_FF // 256):
        gate = conv(slice(256 * g, 256 * (g + 1)))
        val = conv(slice(D_FF + 256 * g, D_FF + 256 * (g + 1)))
        act_s[:, 256 * g:256 * (g + 1)] = (gate * _sigmoid(gate) * val).astype(BF16)

    y = _dot(act_s[...], w_down_ref[...])
    o_ref[...] = x + _rmsnorm(y, post_g_ref[...])


def _x_spec():
    return pl.BlockSpec((None, TQ, D_MODEL), lambda b, t: (b, t, 0))


def _layer_spec(shape, layer):
    return pl.BlockSpec((None,) + shape, lambda b, t: (layer,) + (0,) * len(shape))


_COMPILER_PARAMS = pltpu.CompilerParams(
    dimension_semantics=("arbitrary", "arbitrary"), vmem_limit_bytes=VMEM_LIMIT_BYTES)


def _mixer_call(layer, x, lb_param, w_in, w_out, conv_w, hgrn_g, pre_g, post_g):
    batch, seq, _ = x.shape
    return pl.pallas_call(
        functools.partial(_mixer_kernel, layer),
        out_shape=jax.ShapeDtypeStruct(x.shape, x.dtype),
        grid=(batch, seq // TQ),
        in_specs=[
            _x_spec(),
            pl.BlockSpec((DEPTH, HGRN_WIDTH), lambda b, t: (0, 0)),
            _layer_spec((D_MODEL, IN_COLS), layer),
            _layer_spec((D_MODEL, D_MODEL), layer),
            _layer_spec((3, CONV_WIDTH), layer),
            _layer_spec((1, HGRN_WIDTH), layer),
            _layer_spec((1, D_MODEL), layer),
            _layer_spec((1, D_MODEL), layer),
        ],
        out_specs=_x_spec(),
        scratch_shapes=[
            pltpu.VMEM((TQ, IN_COLS), F32),
            pltpu.VMEM((TQ, HGRN_WIDTH), F32),
            pltpu.VMEM((TQ, HGRN_WIDTH), F32),
            pltpu.VMEM((TQ, HGRN_WIDTH), F32),
            pltpu.VMEM((TQ, HGRN_WIDTH), F32),
            pltpu.VMEM((TQ, D_MODEL), BF16),
            pltpu.VMEM((TQ + HALO, CONV_WIDTH), F32),
            pltpu.VMEM((HEADS, HEAD_DIM, HEAD_DIM), F32),
            pltpu.VMEM((HEADS, CHUNK, HEAD_DIM), F32),
            pltpu.VMEM((HEADS, CHUNK, HEAD_DIM), F32),
            pltpu.VMEM((HEADS, CHUNK, HEAD_DIM), F32),
        ],
        compiler_params=_COMPILER_PARAMS,
        name=f"mixer_l{layer}",
    )(x, lb_param, w_in, w_out, conv_w, hgrn_g, pre_g, post_g)


def _ffn_call(layer, x, w_up, conv_w, w_down, pre_g, post_g):
    batch, seq, _ = x.shape
    return pl.pallas_call(
        _ffn_kernel,
        out_shape=jax.ShapeDtypeStruct(x.shape, x.dtype),
        grid=(batch, seq // TQ),
        in_specs=[
            _x_spec(),
            _layer_spec((D_MODEL, 2 * D_FF), layer),
            _layer_spec((3, 2 * D_FF), layer),
            _layer_spec((D_FF, D_MODEL), layer),
            _layer_spec((1, D_MODEL), layer),
            _layer_spec((1, D_MODEL), layer),
        ],
        out_specs=_x_spec(),
        scratch_shapes=[
            pltpu.VMEM((TQ + HALO, 2 * D_FF), F32),
            pltpu.VMEM((TQ, D_FF), BF16),
        ],
        compiler_params=_COMPILER_PARAMS,
        name=f"ffn_l{layer}",
    )(x, w_up, conv_w, w_down, pre_g, post_g)


def kernel(x, lb_param, w_in, w_out, conv_w, ffn_w_up, ffn_conv_w, ffn_w_down, hgrn_norm_g,
           pre_mix_g, post_mix_g, pre_ffn_g, post_ffn_g):
    assert x.shape[1] % TQ == 0 and x.shape[2] == D_MODEL
    w_in_b = w_in.astype(BF16)
    w_out_b = w_out.astype(BF16)
    w_up_b = ffn_w_up.astype(BF16)
    w_down_b = ffn_w_down.astype(BF16)
    hgrn_g = hgrn_norm_g[:, None, :]
    pre_mix, post_mix = pre_mix_g[:, None, :], post_mix_g[:, None, :]
    pre_ffn, post_ffn = pre_ffn_g[:, None, :], post_ffn_g[:, None, :]
    for layer in range(DEPTH):
        x = _mixer_call(layer, x, lb_param, w_in_b, w_out_b, conv_w, hgrn_g, pre_mix, post_mix)
        x = _ffn_call(layer, x, w_up_b, ffn_conv_w, w_down_b, pre_ffn, post_ffn)
    return x
```

```python
import functools

import jax
import jax.numpy as jnp
from jax import lax
from jax.experimental import pallas as pl
from jax.experimental.pallas import tpu as pltpu

D_MODEL = 1024
DEPTH = 4
HGRN_WIDTH = 512
HEADS = 4
HEAD_DIM = 128
CONV_WIDTH = 512
IN_COLS = 4 * HGRN_WIDTH + 3 * CONV_WIDTH
D_FF = 2816
EPS = 1e-6

TQ = 512
TQ_FFN = 512
SUB_TILE = 256
FCHUNK = 128
MAX_LOG_SPAN = 80.0
CHUNK = 64
SUB = 16
NSUB = CHUNK // SUB
HALO = 8
LANES = 128
FF_BLOCK = 256
NEG_BIG = -1e30
VMEM_LIMIT_BYTES = 56 * 1024 * 1024

F32 = jnp.float32
BF16 = jnp.bfloat16


def _dot(a, b):
    return jnp.dot(a, b, preferred_element_type=F32)


def _dot_nt(a, b):
    return lax.dot_general(a, b, (((1,), (1,)), ((), ())), preferred_element_type=F32)


def _dot_tn(a, b):
    return lax.dot_general(a, b, (((0,), (0,)), ((), ())), preferred_element_type=F32)


def _rmsnorm(x, g):
    ms = jnp.mean(x * x, axis=-1, keepdims=True)
    return x * lax.rsqrt(ms + EPS) * g


def _sigmoid(x):
    return 1.0 / (1.0 + jnp.exp(-x))


def _layer_lower_bound(lb_ref, layer):
    lbp = lb_ref[...]
    m = jnp.max(lbp, axis=0, keepdims=True)
    e = jnp.exp(lbp - m)
    p = e / jnp.sum(e, axis=0, keepdims=True)
    lb = p[1:2]
    for j in range(2, layer + 1):
        lb = lb + p[j:j + 1]
    return lb


def _bcast_row(ref, head, row, nrows):
    return jnp.broadcast_to(ref[head, pl.ds(row, 1), :], (nrows, HEAD_DIM))


def _hgrn_head_offdiag(head, q, k, v, b, st_ref, b_s):
    blk = [slice(SUB * i, SUB * (i + 1)) for i in range(NSUB)]
    bend = [_bcast_row(b_s, head, SUB * i + SUB - 1, SUB) for i in range(NSUB)]
    qp = [q[blk[0]] * jnp.exp(b[blk[0]])]
    for i in range(1, NSUB):
        qp.append(q[blk[i]] * jnp.exp(b[blk[i]] - bend[i - 1]))
    kp = [k[blk[i]] * jnp.exp(bend[i] - b[blk[i]]) for i in range(NSUB)]

    st = st_ref[head]
    st_b = st.astype(BF16)
    qb = [qp[0]] + [qp[i] * jnp.exp(bend[i - 1]) for i in range(1, NSUB)]
    o_inter = _dot_nt(jnp.concatenate(qb, axis=0).astype(BF16), st_b)
    out = [o_inter[blk[i]] for i in range(NSUB)]

    v_b = v.astype(BF16)
    for i in range(1, NSUB):
        parts = [kp[j] * jnp.exp(bend[i - 1] - bend[j]) for j in range(i - 1)] + [kp[i - 1]]
        kcat = jnp.concatenate(parts, axis=0) if len(parts) > 1 else parts[0]
        a = _dot_nt(qp[i].astype(BF16), kcat.astype(BF16))
        out[i] = out[i] + _dot(a.astype(BF16), v_b[0:SUB * i])

    last = NSUB - 1
    kend = jnp.concatenate(
        [kp[j] * jnp.exp(bend[last] - bend[j]) for j in range(last)] + [kp[last]], axis=0)
    decay = jnp.exp(_bcast_row(b_s, head, CHUNK - 1, HEAD_DIM))
    st_ref[head] = st * decay + _dot_tn(v_b, kend.astype(BF16))
    return out


def _hgrn_pair_diag(heads, qs, bs, out, b_s, k_s, v_s, ones_bd):
    row = lax.broadcasted_iota(jnp.int32, (8, HEAD_DIM), 0)
    for i in range(NSUB):
        pieces = {h: [] for h in heads}
        for h, q, b in zip(heads, qs, bs):
            q_lo, q_hi = q[SUB * i:SUB * i + 8], q[SUB * i + 8:SUB * (i + 1)]
            b_lo, b_hi = b[SUB * i:SUB * i + 8], b[SUB * i + 8:SUB * (i + 1)]
            for s in range(SUB):
                r = SUB * i + s
                b_row = _bcast_row(b_s, h, r, 8)
                k_row = _bcast_row(k_s, h, r, 8)
                if s < 8:
                    d_lo = jnp.where(row >= s, b_lo - b_row, NEG_BIG)
                    pieces[h].append(q_lo * jnp.exp(d_lo) * k_row)
                    pieces[h].append(q_hi * jnp.exp(b_hi - b_row) * k_row)
                else:
                    d_hi = jnp.where(row >= s - 8, b_hi - b_row, NEG_BIG)
                    pieces[h].append(q_hi * jnp.exp(d_hi) * k_row)
        p = jnp.concatenate(
            [jnp.concatenate(pieces[h], axis=0) for h in heads], axis=1).astype(BF16)
        red = _dot(p, ones_bd)
        for hi, h in enumerate(heads):
            lanes = slice(HEAD_DIM * hi, HEAD_DIM * (hi + 1))
            acc_lo = out[h][i][0:8]
            acc_hi = out[h][i][8:SUB]
            n = 0
            for s in range(SUB):
                v_row = _bcast_row(v_s, h, SUB * i + s, 8)
                if s < 8:
                    acc_lo = acc_lo + red[8 * n:8 * n + 8, lanes] * v_row
                    n += 1
                acc_hi = acc_hi + red[8 * n:8 * n + 8, lanes] * v_row
                n += 1
            out[h][i] = jnp.concatenate([acc_lo, acc_hi], axis=0)


def _decay_consts(layer, lb_ref):
    if layer == 0:
        return None
    lb = _layer_lower_bound(lb_ref, layer)
    return jnp.log(lb), jnp.log1p(-lb), 1.0 - lb


def _gate_prep(consts, r0, z_s, q_s, k_s, g_s, b_s, tri):
    rows = slice(r0, r0 + SUB_TILE)
    for hd in range(HEADS):
        lanes = slice(HEAD_DIM * hd, HEAD_DIM * (hd + 1))
        fp = z_s[rows, HGRN_WIDTH + HEAD_DIM * hd:HGRN_WIDTH + HEAD_DIM * (hd + 1)]
        e = jnp.exp(-jnp.abs(fp))
        log_sig = jnp.minimum(fp, 0.0) - jnp.log(1.0 + e)
        r = 1.0 / (1.0 + e)
        sig_neg = jnp.where(fp >= 0.0, e * r, r)
        if consts is None:
            log_f, kk = log_sig, sig_neg
        else:
            log_lb, log_1m, one_m = consts
            cc = log_1m[:, lanes] + log_sig
            la = log_lb[:, lanes]
            log_f = jnp.maximum(la, cc) + jnp.log(1.0 + jnp.exp(-jnp.abs(la - cc)))
            kk = one_m[:, lanes] * sig_neg
        k_s[rows, lanes] = kk
        b_s[rows, lanes] = log_f
        qz = z_s[rows, lanes]
        q_s[rows, lanes] = qz * _sigmoid(qz) * (HEAD_DIM ** -0.5)
        gz = z_s[rows, 3 * HGRN_WIDTH + HEAD_DIM * hd:3 * HGRN_WIDTH + HEAD_DIM * (hd + 1)]
        g_s[rows, lanes] = gz * _sigmoid(gz)

    span = None
    for c in range(SUB_TILE // FCHUNK):
        crows = slice(r0 + FCHUNK * c, r0 + FCHUNK * (c + 1))
        log_f = b_s[crows, :]
        g1 = log_f.astype(BF16)
        r1 = log_f - g1.astype(F32)
        g2 = r1.astype(BF16)
        g3 = (r1 - g2.astype(F32)).astype(BF16)
        b = _dot(tri, g1) + _dot(tri, g2) + _dot(tri, g3)
        b_s[crows, :] = b
        mid = b[FCHUNK // 2 - 1:FCHUNK // 2]
        s = jnp.maximum(b[0:1] - mid, mid - b[FCHUNK - 1:FCHUNK])
        span = s if span is None else jnp.maximum(span, s)
    return span


def _hgrn_scores(c0, hd, z_s, q_s, k_s, b_s):
    lanes = slice(HEAD_DIM * hd, HEAD_DIM * (hd + 1))
    rows = slice(c0, c0 + FCHUNK)
    b = b_s[rows, lanes]
    mid = b_s[c0 + FCHUNK // 2 - 1:c0 + FCHUNK // 2, lanes]
    last = b_s[c0 + FCHUNK - 1:c0 + FCHUNK, lanes]
    qs = q_s[rows, lanes] * jnp.exp(b - mid)
    ks = k_s[rows, lanes] * jnp.exp(mid - b)
    a = _dot_nt(qs.astype(BF16), ks.astype(BF16))
    qe = (qs * jnp.exp(mid)).astype(BF16)
    ke = (ks * jnp.exp(last - mid)).astype(BF16)
    return a, qe, ke, jnp.exp(last)


def _hgrn_outputs(c0, hd, st, scores, z_s, g_s, hg, mix_s, tril):
    a, qe, ke, decay = scores
    lanes = slice(HEAD_DIM * hd, HEAD_DIM * (hd + 1))
    rows = slice(c0, c0 + FCHUNK)
    v = z_s[rows, 2 * HGRN_WIDTH + HEAD_DIM * hd:2 * HGRN_WIDTH + HEAD_DIM * (hd + 1)]
    v = v.astype(BF16)
    a = jnp.where(tril, a, 0.0).astype(BF16)
    o = _dot(a, v) + _dot_nt(qe, st.astype(BF16))
    st = st * decay + _dot_tn(v, ke)
    ms = jnp.mean(o * o, axis=-1, keepdims=True)
    on = o * lax.rsqrt(ms + EPS) * hg[:, lanes] * g_s[rows, lanes]
    mix_s[rows, lanes] = on.astype(BF16)
    return st


def _hgrn_any_range(z_s, q_s, k_s, g_s, b_s, hg, st_s, mix_s, cb_s, ck_s, cv_s):
    ones_bd = jnp.where(
        lax.broadcasted_iota(jnp.int32, (2 * HEAD_DIM, 2 * HEAD_DIM), 0) // HEAD_DIM
        == lax.broadcasted_iota(jnp.int32, (2 * HEAD_DIM, 2 * HEAD_DIM), 1) // HEAD_DIM,
        1.0, 0.0).astype(BF16)

    def chunk_body(c, carry):
        for half in range(FCHUNK // CHUNK):
            r0 = pl.multiple_of(c * FCHUNK + CHUNK * half, CHUNK)
            rows = pl.ds(r0, CHUNK)
            b = b_s[rows, :]
            if half:
                b = b - b_s[pl.ds(r0 - 1, 1), :]
            q = q_s[rows, :]
            kk = k_s[rows, :]
            v = z_s[rows, 2 * HGRN_WIDTH:3 * HGRN_WIDTH]
            gate = g_s[rows, :]
            cb_s[...] = jnp.stack([b[:, HEAD_DIM * hd:HEAD_DIM * (hd + 1)] for hd in range(HEADS)])
            ck_s[...] = jnp.stack([kk[:, HEAD_DIM * hd:HEAD_DIM * (hd + 1)] for hd in range(HEADS)])
            cv_s[...] = jnp.stack([v[:, HEAD_DIM * hd:HEAD_DIM * (hd + 1)] for hd in range(HEADS)])
            out = {}
            for hd in range(HEADS):
                lanes = slice(HEAD_DIM * hd, HEAD_DIM * (hd + 1))
                out[hd] = _hgrn_head_offdiag(hd, q[:, lanes], kk[:, lanes], v[:, lanes],
                                             b[:, lanes], st_s, cb_s)
            for pair in ((0, 1), (2, 3)):
                _hgrn_pair_diag(pair,
                                [q[:, HEAD_DIM * hd:HEAD_DIM * (hd + 1)] for hd in pair],
                                [b[:, HEAD_DIM * hd:HEAD_DIM * (hd + 1)] for hd in pair],
                                out, cb_s, ck_s, cv_s, ones_bd)
            for hd in range(HEADS):
                lanes = slice(HEAD_DIM * hd, HEAD_DIM * (hd + 1))
                oh = jnp.concatenate(out[hd], axis=0)
                ms = jnp.mean(oh * oh, axis=-1, keepdims=True)
                on = oh * lax.rsqrt(ms + EPS) * hg[:, lanes] * gate[:, lanes]
                mix_s[rows, lanes] = on.astype(BF16)
        return carry

    lax.fori_loop(0, TQ // FCHUNK, chunk_body, 0)


GATE_GROUPS = (1, 0, 3)
OTHER_GROUPS = (2, 5, 6, 4)


def _mixer_kernel(layer, x_ref, lb_ref, w_in_ref, w_out_ref, cw_ref, hg_ref, pre_g_ref,
                  post_g_ref, o_ref, h_s, z_s, q_s, k_s, g_s, b_s, mix_s, cbuf, st_s, st_next_s,
                  cb_s, ck_s, cv_s):
    t = pl.program_id(1)
    subs = [SUB_TILE * i for i in range(TQ // SUB_TILE)]

    @pl.when(t == 0)
    def _():
        st_s[...] = jnp.zeros_like(st_s)
        cbuf[:, pl.ds(TQ, HALO), :] = jnp.zeros((CONV_WIDTH // LANES, HALO, LANES), F32)

    cbuf[:, 0:HALO, :] = cbuf[:, TQ:TQ + HALO, :]

    for r0 in subs:
        rows = slice(r0, r0 + SUB_TILE)
        h_s[rows, :] = _rmsnorm(x_ref[rows, :], pre_g_ref[...]).astype(BF16)
    for groups in (GATE_GROUPS, OTHER_GROUPS):
        for r0 in subs:
            rows = slice(r0, r0 + SUB_TILE)
            for g in groups:
                cols = slice(512 * g, 512 * (g + 1))
                z_s[rows, cols] = _dot(h_s[rows, :], w_in_ref[:, cols])

    consts = _decay_consts(layer, lb_ref)
    tril = (lax.broadcasted_iota(jnp.int32, (FCHUNK, FCHUNK), 0)
            >= lax.broadcasted_iota(jnp.int32, (FCHUNK, FCHUNK), 1))
    tri = jnp.where(tril, 1.0, 0.0).astype(BF16)
    hg = hg_ref[...]
    cw = cw_ref[...]

    def out_project(r0):
        rows = slice(r0, r0 + SUB_TILE)
        mix = _dot(mix_s[rows, :], w_out_ref[...])
        o_ref[rows, :] = x_ref[rows, :] + _rmsnorm(mix, post_g_ref[...])

    state = [st_s[hd] for hd in range(HEADS)]
    span = None
    for r0 in subs:
        rows = slice(r0, r0 + SUB_TILE)
        s = _gate_prep(consts, r0, z_s, q_s, k_s, g_s, b_s, tri)
        span = s if span is None else jnp.maximum(span, s)
        for hd in range(HEADS):
            for c in range(SUB_TILE // FCHUNK):
                c0 = r0 + FCHUNK * c
                scores = _hgrn_scores(c0, hd, z_s, q_s, k_s, b_s)
                state[hd] = _hgrn_outputs(c0, hd, state[hd], scores, z_s, g_s, hg, mix_s, tril)
        lo = HALO + r0
        for j in range(CONV_WIDTH // LANES):
            lanes = slice(LANES * j, LANES * (j + 1))
            cbuf[j, lo:lo + SUB_TILE, :] = (z_s[rows, 2560 + LANES * j:2560 + LANES * (j + 1)]
                                            * z_s[rows, 3072 + LANES * j:3072 + LANES * (j + 1)])
            y = (cw[2:3, lanes] * cbuf[j, lo:lo + SUB_TILE, :]
                 + cw[1:2, lanes] * cbuf[j, lo - 1:lo - 1 + SUB_TILE, :]
                 + cw[0:1, lanes] * cbuf[j, lo - 2:lo - 2 + SUB_TILE, :])
            gate_b = z_s[rows, 2048 + LANES * j:2048 + LANES * (j + 1)]
            mix_s[rows, HGRN_WIDTH + LANES * j:HGRN_WIDTH + LANES * (j + 1)] = (
                gate_b * y).astype(BF16)
        out_project(r0)
    for hd in range(HEADS):
        st_next_s[hd] = state[hd]

    @pl.when(jnp.logical_not(jnp.max(span) <= MAX_LOG_SPAN))
    def _():
        st_next_s[...] = st_s[...]
        _hgrn_any_range(z_s, q_s, k_s, g_s, b_s, hg, st_next_s, mix_s, cb_s, ck_s, cv_s)
        for r0 in subs:
            out_project(r0)

    st_s[...] = st_next_s[...]


def _ffn_kernel(x_ref, w_up_ref, cw_ref, w_down_ref, pre_g_ref, post_g_ref, o_ref, ubuf):
    t = pl.program_id(1)

    @pl.when(t == 0)
    def _():
        ubuf[:, pl.ds(TQ_FFN, HALO), :] = jnp.zeros((2 * D_FF // LANES, HALO, LANES), F32)

    ubuf[:, 0:HALO, :] = ubuf[:, TQ_FFN:TQ_FFN + HALO, :]
    per_block = FF_BLOCK // LANES
    n_blocks = D_FF // FF_BLOCK

    for sub in range(TQ_FFN // SUB_TILE):
        r0 = SUB_TILE * sub
        x = x_ref[r0:r0 + SUB_TILE, :]
        h = _rmsnorm(x, pre_g_ref[...]).astype(BF16)

        def conv(slab):
            w = cw_ref[:, LANES * slab:LANES * (slab + 1)]
            lo = HALO + r0
            return (w[2:3] * ubuf[slab, lo:lo + SUB_TILE, :]
                    + w[1:2] * ubuf[slab, lo - 1:lo - 1 + SUB_TILE, :]
                    + w[0:1] * ubuf[slab, lo - 2:lo - 2 + SUB_TILE, :])

        def up_project(g):
            for base in (FF_BLOCK * g, D_FF + FF_BLOCK * g):
                u = _dot(h, w_up_ref[:, base:base + FF_BLOCK])
                for j in range(per_block):
                    ubuf[base // LANES + j, HALO + r0:HALO + r0 + SUB_TILE, :] = (
                        u[:, LANES * j:LANES * (j + 1)])

        def activate(g):
            act = []
            for j in range(per_block):
                gate = conv(FF_BLOCK * g // LANES + j)
                val = conv((D_FF + FF_BLOCK * g) // LANES + j)
                act.append((gate * _sigmoid(gate) * val).astype(BF16))
            return jnp.concatenate(act, axis=1)

        def down_project(y, g, act):
            part = _dot(act, w_down_ref[FF_BLOCK * g:FF_BLOCK * (g + 1), :])
            return part if y is None else y + part

        y = None
        prev_act = None
        up_project(0)
        for g in range(n_blocks):
            if g + 1 < n_blocks:
                up_project(g + 1)
            if g >= 1:
                y = down_project(y, g - 1, prev_act)
            prev_act = activate(g)
        y = down_project(y, n_blocks - 1, prev_act)

        o_ref[r0:r0 + SUB_TILE, :] = x + _rmsnorm(y, post_g_ref[...])


def _x_spec(tq):
    return pl.BlockSpec((None, tq, D_MODEL), lambda b, t: (b, t, 0))


def _layer_spec(shape, layer):
    return pl.BlockSpec((None,) + shape, lambda b, t: (layer,) + (0,) * len(shape),
                        pipeline_mode=pl.Buffered(1))


_COMPILER_PARAMS = pltpu.CompilerParams(
    dimension_semantics=("arbitrary", "arbitrary"), vmem_limit_bytes=VMEM_LIMIT_BYTES)


def _mixer_call(layer, x, lb_param, w_in, w_out, conv_w, hgrn_g, pre_g, post_g):
    batch, seq, _ = x.shape
    return pl.pallas_call(
        functools.partial(_mixer_kernel, layer),
        out_shape=jax.ShapeDtypeStruct(x.shape, x.dtype),
        grid=(batch, seq // TQ),
        in_specs=[
            _x_spec(TQ),
            pl.BlockSpec((DEPTH, HGRN_WIDTH), lambda b, t: (0, 0)),
            _layer_spec((D_MODEL, IN_COLS), layer),
            _layer_spec((D_MODEL, D_MODEL), layer),
            _layer_spec((3, CONV_WIDTH), layer),
            _layer_spec((1, HGRN_WIDTH), layer),
            _layer_spec((1, D_MODEL), layer),
            _layer_spec((1, D_MODEL), layer),
        ],
        out_specs=_x_spec(TQ),
        scratch_shapes=[
            pltpu.VMEM((TQ, D_MODEL), BF16),
            pltpu.VMEM((TQ, IN_COLS), F32),
            pltpu.VMEM((TQ, HGRN_WIDTH), F32),
            pltpu.VMEM((TQ, HGRN_WIDTH), F32),
            pltpu.VMEM((TQ, HGRN_WIDTH), F32),
            pltpu.VMEM((TQ, HGRN_WIDTH), F32),
            pltpu.VMEM((TQ, D_MODEL), BF16),
            pltpu.VMEM((CONV_WIDTH // LANES, TQ + HALO, LANES), F32),
            pltpu.VMEM((HEADS, HEAD_DIM, HEAD_DIM), F32),
            pltpu.VMEM((HEADS, HEAD_DIM, HEAD_DIM), F32),
            pltpu.VMEM((HEADS, CHUNK, HEAD_DIM), F32),
            pltpu.VMEM((HEADS, CHUNK, HEAD_DIM), F32),
            pltpu.VMEM((HEADS, CHUNK, HEAD_DIM), F32),
        ],
        compiler_params=_COMPILER_PARAMS,
        name=f"mixer_l{layer}",
    )(x, lb_param, w_in, w_out, conv_w, hgrn_g, pre_g, post_g)


def _ffn_call(layer, x, w_up, conv_w, w_down, pre_g, post_g):
    batch, seq, _ = x.shape
    return pl.pallas_call(
        _ffn_kernel,
        out_shape=jax.ShapeDtypeStruct(x.shape, x.dtype),
        grid=(batch, seq // TQ_FFN),
        in_specs=[
            _x_spec(TQ_FFN),
            _layer_spec((D_MODEL, 2 * D_FF), layer),
            _layer_spec((3, 2 * D_FF), layer),
            _layer_spec((D_FF, D_MODEL), layer),
            _layer_spec((1, D_MODEL), layer),
            _layer_spec((1, D_MODEL), layer),
        ],
        out_specs=_x_spec(TQ_FFN),
        scratch_shapes=[
            pltpu.VMEM((2 * D_FF // LANES, TQ_FFN + HALO, LANES), F32),
        ],
        compiler_params=_COMPILER_PARAMS,
        name=f"ffn_l{layer}",
    )(x, w_up, conv_w, w_down, pre_g, post_g)


def kernel(x, lb_param, w_in, w_out, conv_w, ffn_w_up, ffn_conv_w, ffn_w_down, hgrn_norm_g,
           pre_mix_g, post_mix_g, pre_ffn_g, post_ffn_g):
    assert x.shape[1] % TQ == 0 and x.shape[1] % TQ_FFN == 0 and x.shape[2] == D_MODEL
    w_in_b = w_in.astype(BF16)
    w_out_b = w_out.astype(BF16)
    w_up_b = ffn_w_up.astype(BF16)
    w_down_b = ffn_w_down.astype(BF16)
    hgrn_g = hgrn_norm_g[:, None, :]
    pre_mix, post_mix = pre_mix_g[:, None, :], post_mix_g[:, None, :]
    pre_ffn, post_ffn = pre_ffn_g[:, None, :], post_ffn_g[:, None, :]
    for layer in range(DEPTH):
        x = _mixer_call(layer, x, lb_param, w_in_b, w_out_b, conv_w, hgrn_g, pre_mix, post_mix)
        x = _ffn_call(layer, x, w_up_b, ffn_conv_w, w_down_b, pre_ffn, post_ffn)
    return x
```

```python
import functools

import jax
import jax.numpy as jnp
from jax import lax
from jax.experimental import pallas as pl
from jax.experimental.pallas import tpu as pltpu

D_MODEL = 1024
DEPTH = 4
HGRN_WIDTH = 512
HEADS = 4
HEAD_DIM = 128
CONV_WIDTH = 512
IN_COLS = 4 * HGRN_WIDTH + 3 * CONV_WIDTH
D_FF = 2816
EPS = 1e-6

TQ = 512
TQ_FFN = 512
SUB_TILE = 256
FCHUNK = 128
MAX_LOG_SPAN = 80.0
CHUNK = 64
SUB = 16
NSUB = CHUNK // SUB
HALO = 8
LANES = 128
FF_BLOCK = 256
NEG_BIG = -1e30
W_STAGE = (1024, 512)
W_STAGE_DOWN = (256, 1024)
VMEM_LIMIT_BYTES = 56 * 1024 * 1024

F32 = jnp.float32
BF16 = jnp.bfloat16


def _dot(a, b):
    return jnp.dot(a, b, preferred_element_type=F32)


def _dot_nt(a, b):
    return lax.dot_general(a, b, (((1,), (1,)), ((), ())), preferred_element_type=F32)


def _dot_tn(a, b):
    return lax.dot_general(a, b, (((0,), (0,)), ((), ())), preferred_element_type=F32)


def _rmsnorm(x, g):
    ms = jnp.mean(x * x, axis=-1, keepdims=True)
    return x * lax.rsqrt(ms + EPS) * g


def _sigmoid(x):
    return 1.0 / (1.0 + jnp.exp(-x))


def _first_grid_step():
    return jnp.logical_and(pl.program_id(0) == 0, pl.program_id(1) == 0)


def _load_weight_bf16(w_hbm, layer, w_bf, stage, sem, row_chunk, col_chunk):
    rows, cols = w_bf.shape
    for r in range(rows // row_chunk):
        for c in range(cols // col_chunk):
            src = w_hbm.at[layer, pl.ds(row_chunk * r, row_chunk), pl.ds(col_chunk * c, col_chunk)]
            copy = pltpu.make_async_copy(src, stage, sem.at[0])
            copy.start()
            copy.wait()
            w_bf[row_chunk * r:row_chunk * (r + 1), col_chunk * c:col_chunk * (c + 1)] = (
                stage[...].astype(BF16))


def _layer_lower_bound(lb_ref, layer):
    lbp = lb_ref[...]
    m = jnp.max(lbp, axis=0, keepdims=True)
    e = jnp.exp(lbp - m)
    p = e / jnp.sum(e, axis=0, keepdims=True)
    lb = p[1:2]
    for j in range(2, layer + 1):
        lb = lb + p[j:j + 1]
    return lb


def _bcast_row(ref, head, row, nrows):
    return jnp.broadcast_to(ref[head, pl.ds(row, 1), :], (nrows, HEAD_DIM))


def _hgrn_head_offdiag(head, q, k, v, b, st_ref, b_s):
    blk = [slice(SUB * i, SUB * (i + 1)) for i in range(NSUB)]
    bend = [_bcast_row(b_s, head, SUB * i + SUB - 1, SUB) for i in range(NSUB)]
    qp = [q[blk[0]] * jnp.exp(b[blk[0]])]
    for i in range(1, NSUB):
        qp.append(q[blk[i]] * jnp.exp(b[blk[i]] - bend[i - 1]))
    kp = [k[blk[i]] * jnp.exp(bend[i] - b[blk[i]]) for i in range(NSUB)]

    st = st_ref[head]
    st_b = st.astype(BF16)
    qb = [qp[0]] + [qp[i] * jnp.exp(bend[i - 1]) for i in range(1, NSUB)]
    o_inter = _dot_nt(jnp.concatenate(qb, axis=0).astype(BF16), st_b)
    out = [o_inter[blk[i]] for i in range(NSUB)]

    v_b = v.astype(BF16)
    for i in range(1, NSUB):
        parts = [kp[j] * jnp.exp(bend[i - 1] - bend[j]) for j in range(i - 1)] + [kp[i - 1]]
        kcat = jnp.concatenate(parts, axis=0) if len(parts) > 1 else parts[0]
        a = _dot_nt(qp[i].astype(BF16), kcat.astype(BF16))
        out[i] = out[i] + _dot(a.astype(BF16), v_b[0:SUB * i])

    last = NSUB - 1
    kend = jnp.concatenate(
        [kp[j] * jnp.exp(bend[last] - bend[j]) for j in range(last)] + [kp[last]], axis=0)
    decay = jnp.exp(_bcast_row(b_s, head, CHUNK - 1, HEAD_DIM))
    st_ref[head] = st * decay + _dot_tn(v_b, kend.astype(BF16))
    return out


def _hgrn_pair_diag(heads, qs, bs, out, b_s, k_s, v_s, ones_bd):
    row = lax.broadcasted_iota(jnp.int32, (8, HEAD_DIM), 0)
    for i in range(NSUB):
        pieces = {h: [] for h in heads}
        for h, q, b in zip(heads, qs, bs):
            q_lo, q_hi = q[SUB * i:SUB * i + 8], q[SUB * i + 8:SUB * (i + 1)]
            b_lo, b_hi = b[SUB * i:SUB * i + 8], b[SUB * i + 8:SUB * (i + 1)]
            for s in range(SUB):
                r = SUB * i + s
                b_row = _bcast_row(b_s, h, r, 8)
                k_row = _bcast_row(k_s, h, r, 8)
                if s < 8:
                    d_lo = jnp.where(row >= s, b_lo - b_row, NEG_BIG)
                    pieces[h].append(q_lo * jnp.exp(d_lo) * k_row)
                    pieces[h].append(q_hi * jnp.exp(b_hi - b_row) * k_row)
                else:
                    d_hi = jnp.where(row >= s - 8, b_hi - b_row, NEG_BIG)
                    pieces[h].append(q_hi * jnp.exp(d_hi) * k_row)
        p = jnp.concatenate(
            [jnp.concatenate(pieces[h], axis=0) for h in heads], axis=1).astype(BF16)
        red = _dot(p, ones_bd)
        for hi, h in enumerate(heads):
            lanes = slice(HEAD_DIM * hi, HEAD_DIM * (hi + 1))
            acc_lo = out[h][i][0:8]
            acc_hi = out[h][i][8:SUB]
            n = 0
            for s in range(SUB):
                v_row = _bcast_row(v_s, h, SUB * i + s, 8)
                if s < 8:
                    acc_lo = acc_lo + red[8 * n:8 * n + 8, lanes] * v_row
                    n += 1
                acc_hi = acc_hi + red[8 * n:8 * n + 8, lanes] * v_row
                n += 1
            out[h][i] = jnp.concatenate([acc_lo, acc_hi], axis=0)


def _decay_consts(layer, lb_ref):
    if layer == 0:
        return None
    lb = _layer_lower_bound(lb_ref, layer)
    return jnp.log(lb), jnp.log1p(-lb), 1.0 - lb


def _gate_prep(consts, r0, z_s, q_s, k_s, g_s, b_s, tri):
    rows = slice(r0, r0 + SUB_TILE)
    for hd in range(HEADS):
        lanes = slice(HEAD_DIM * hd, HEAD_DIM * (hd + 1))
        fp = z_s[rows, HGRN_WIDTH + HEAD_DIM * hd:HGRN_WIDTH + HEAD_DIM * (hd + 1)]
        e = jnp.exp(-jnp.abs(fp))
        log_sig = jnp.minimum(fp, 0.0) - jnp.log(1.0 + e)
        r = 1.0 / (1.0 + e)
        sig_neg = jnp.where(fp >= 0.0, e * r, r)
        if consts is None:
            log_f, kk = log_sig, sig_neg
        else:
            log_lb, log_1m, one_m = consts
            cc = log_1m[:, lanes] + log_sig
            la = log_lb[:, lanes]
            log_f = jnp.maximum(la, cc) + jnp.log(1.0 + jnp.exp(-jnp.abs(la - cc)))
            kk = one_m[:, lanes] * sig_neg
        k_s[rows, lanes] = kk
        b_s[rows, lanes] = log_f
        qz = z_s[rows, lanes]
        q_s[rows, lanes] = qz * _sigmoid(qz) * (HEAD_DIM ** -0.5)
        gz = z_s[rows, 3 * HGRN_WIDTH + HEAD_DIM * hd:3 * HGRN_WIDTH + HEAD_DIM * (hd + 1)]
        g_s[rows, lanes] = gz * _sigmoid(gz)

    span = None
    for c in range(SUB_TILE // FCHUNK):
        crows = slice(r0 + FCHUNK * c, r0 + FCHUNK * (c + 1))
        log_f = b_s[crows, :]
        g1 = log_f.astype(BF16)
        r1 = log_f - g1.astype(F32)
        g2 = r1.astype(BF16)
        g3 = (r1 - g2.astype(F32)).astype(BF16)
        b = _dot(tri, g1) + _dot(tri, g2) + _dot(tri, g3)
        b_s[crows, :] = b
        mid = b[FCHUNK // 2 - 1:FCHUNK // 2]
        s = jnp.maximum(b[0:1] - mid, mid - b[FCHUNK - 1:FCHUNK])
        span = s if span is None else jnp.maximum(span, s)
    return span


def _hgrn_scores(c0, hd, z_s, q_s, k_s, b_s):
    lanes = slice(HEAD_DIM * hd, HEAD_DIM * (hd + 1))
    rows = slice(c0, c0 + FCHUNK)
    b = b_s[rows, lanes]
    mid = b_s[c0 + FCHUNK // 2 - 1:c0 + FCHUNK // 2, lanes]
    last = b_s[c0 + FCHUNK - 1:c0 + FCHUNK, lanes]
    v = z_s[rows, 2 * HGRN_WIDTH + HEAD_DIM * hd:2 * HGRN_WIDTH + HEAD_DIM * (hd + 1)]
    v = v.astype(BF16)
    qs = q_s[rows, lanes] * jnp.exp(b - mid)
    ks = k_s[rows, lanes] * jnp.exp(mid - b)
    a = _dot_nt(qs.astype(BF16), ks.astype(BF16))
    qe = (qs * jnp.exp(mid)).astype(BF16)
    ke = (ks * jnp.exp(last - mid)).astype(BF16)
    return a, qe, _dot_tn(v, ke), jnp.exp(last)


def _hgrn_outputs(c0, hd, st, scores, z_s, g_s, hg, mix_s, tril):
    a, qe, kv, decay = scores
    lanes = slice(HEAD_DIM * hd, HEAD_DIM * (hd + 1))
    rows = slice(c0, c0 + FCHUNK)
    v = z_s[rows, 2 * HGRN_WIDTH + HEAD_DIM * hd:2 * HGRN_WIDTH + HEAD_DIM * (hd + 1)]
    v = v.astype(BF16)
    a = jnp.where(tril, a, 0.0).astype(BF16)
    o = _dot(a, v) + _dot_nt(qe, st.astype(BF16))
    st = st * decay + kv
    ms = jnp.mean(o * o, axis=-1, keepdims=True)
    on = o * lax.rsqrt(ms + EPS) * hg[:, lanes] * g_s[rows, lanes]
    mix_s[rows, lanes] = on.astype(BF16)
    return st


def _hgrn_any_range(z_s, q_s, k_s, g_s, b_s, hg, st_s, mix_s, cb_s, ck_s, cv_s):
    ones_bd = jnp.where(
        lax.broadcasted_iota(jnp.int32, (2 * HEAD_DIM, 2 * HEAD_DIM), 0) // HEAD_DIM
        == lax.broadcasted_iota(jnp.int32, (2 * HEAD_DIM, 2 * HEAD_DIM), 1) // HEAD_DIM,
        1.0, 0.0).astype(BF16)

    def chunk_body(c, carry):
        for half in range(FCHUNK // CHUNK):
            r0 = pl.multiple_of(c * FCHUNK + CHUNK * half, CHUNK)
            rows = pl.ds(r0, CHUNK)
            b = b_s[rows, :]
            if half:
                b = b - b_s[pl.ds(r0 - 1, 1), :]
            q = q_s[rows, :]
            kk = k_s[rows, :]
            v = z_s[rows, 2 * HGRN_WIDTH:3 * HGRN_WIDTH]
            gate = g_s[rows, :]
            cb_s[...] = jnp.stack([b[:, HEAD_DIM * hd:HEAD_DIM * (hd + 1)] for hd in range(HEADS)])
            ck_s[...] = jnp.stack([kk[:, HEAD_DIM * hd:HEAD_DIM * (hd + 1)] for hd in range(HEADS)])
            cv_s[...] = jnp.stack([v[:, HEAD_DIM * hd:HEAD_DIM * (hd + 1)] for hd in range(HEADS)])
            out = {}
            for hd in range(HEADS):
                lanes = slice(HEAD_DIM * hd, HEAD_DIM * (hd + 1))
                out[hd] = _hgrn_head_offdiag(hd, q[:, lanes], kk[:, lanes], v[:, lanes],
                                             b[:, lanes], st_s, cb_s)
            for pair in ((0, 1), (2, 3)):
                _hgrn_pair_diag(pair,
                                [q[:, HEAD_DIM * hd:HEAD_DIM * (hd + 1)] for hd in pair],
                                [b[:, HEAD_DIM * hd:HEAD_DIM * (hd + 1)] for hd in pair],
                                out, cb_s, ck_s, cv_s, ones_bd)
            for hd in range(HEADS):
                lanes = slice(HEAD_DIM * hd, HEAD_DIM * (hd + 1))
                oh = jnp.concatenate(out[hd], axis=0)
                ms = jnp.mean(oh * oh, axis=-1, keepdims=True)
                on = oh * lax.rsqrt(ms + EPS) * hg[:, lanes] * gate[:, lanes]
                mix_s[rows, lanes] = on.astype(BF16)
        return carry

    lax.fori_loop(0, TQ // FCHUNK, chunk_body, 0)


GATE_GROUPS = (1, 0, 3)
OTHER_GROUPS = (2, 5, 6, 4)


def _mixer_kernel(layer, x_ref, lb_ref, w_in_hbm, w_out_hbm, cw_ref, hg_ref, pre_g_ref,
                  post_g_ref, o_ref, w_in_ref, w_out_ref, w_stage, w_sem, h_s, z_s, q_s, k_s, g_s,
                  b_s, mix_s, cbuf, st_s, st_next_s, cb_s, ck_s, cv_s):
    t = pl.program_id(1)
    subs = [SUB_TILE * i for i in range(TQ // SUB_TILE)]

    @pl.when(_first_grid_step())
    def _():
        _load_weight_bf16(w_in_hbm, layer, w_in_ref, w_stage, w_sem, *w_stage.shape)
        _load_weight_bf16(w_out_hbm, layer, w_out_ref, w_stage, w_sem, *w_stage.shape)

    @pl.when(t == 0)
    def _():
        st_s[...] = jnp.zeros_like(st_s)
        cbuf[:, pl.ds(TQ, HALO), :] = jnp.zeros((CONV_WIDTH // LANES, HALO, LANES), F32)

    cbuf[:, 0:HALO, :] = cbuf[:, TQ:TQ + HALO, :]

    def in_project(r0):
        rows = slice(r0, r0 + SUB_TILE)
        h_s[rows, :] = _rmsnorm(x_ref[rows, :], pre_g_ref[...]).astype(BF16)
        for g in GATE_GROUPS + OTHER_GROUPS:
            cols = slice(512 * g, 512 * (g + 1))
            z_s[rows, cols] = _dot(h_s[rows, :], w_in_ref[:, cols])

    consts = _decay_consts(layer, lb_ref)
    tril = (lax.broadcasted_iota(jnp.int32, (FCHUNK, FCHUNK), 0)
            >= lax.broadcasted_iota(jnp.int32, (FCHUNK, FCHUNK), 1))
    tri = jnp.where(tril, 1.0, 0.0).astype(BF16)
    hg = hg_ref[...]
    cw = cw_ref[...]

    def out_project(r0):
        rows = slice(r0, r0 + SUB_TILE)
        mix = _dot(mix_s[rows, :], w_out_ref[...])
        o_ref[rows, :] = x_ref[rows, :] + _rmsnorm(mix, post_g_ref[...])

    state = [st_s[hd] for hd in range(HEADS)]
    span = None
    in_project(subs[0])
    for i, r0 in enumerate(subs):
        rows = slice(r0, r0 + SUB_TILE)
        s = _gate_prep(consts, r0, z_s, q_s, k_s, g_s, b_s, tri)
        span = s if span is None else jnp.maximum(span, s)
        chunk_rows = [r0 + FCHUNK * c for c in range(SUB_TILE // FCHUNK)]
        scores = {(hd, c0): _hgrn_scores(c0, hd, z_s, q_s, k_s, b_s)
                  for hd in range(HEADS) for c0 in chunk_rows}
        for hd in range(HEADS):
            for c0 in chunk_rows:
                state[hd] = _hgrn_outputs(c0, hd, state[hd], scores[hd, c0], z_s, g_s, hg,
                                          mix_s, tril)
        lo = HALO + r0
        for j in range(CONV_WIDTH // LANES):
            lanes = slice(LANES * j, LANES * (j + 1))
            cbuf[j, lo:lo + SUB_TILE, :] = (z_s[rows, 2560 + LANES * j:2560 + LANES * (j + 1)]
                                            * z_s[rows, 3072 + LANES * j:3072 + LANES * (j + 1)])
            y = (cw[2:3, lanes] * cbuf[j, lo:lo + SUB_TILE, :]
                 + cw[1:2, lanes] * cbuf[j, lo - 1:lo - 1 + SUB_TILE, :]
                 + cw[0:1, lanes] * cbuf[j, lo - 2:lo - 2 + SUB_TILE, :])
            gate_b = z_s[rows, 2048 + LANES * j:2048 + LANES * (j + 1)]
            mix_s[rows, HGRN_WIDTH + LANES * j:HGRN_WIDTH + LANES * (j + 1)] = (
                gate_b * y).astype(BF16)
        if i + 1 < len(subs):
            in_project(subs[i + 1])
        if i >= 1:
            out_project(subs[i - 1])
    out_project(subs[-1])
    for hd in range(HEADS):
        st_next_s[hd] = state[hd]

    @pl.when(jnp.logical_not(jnp.max(span) <= MAX_LOG_SPAN))
    def _():
        st_next_s[...] = st_s[...]
        _hgrn_any_range(z_s, q_s, k_s, g_s, b_s, hg, st_next_s, mix_s, cb_s, ck_s, cv_s)
        for r0 in subs:
            out_project(r0)

    st_s[...] = st_next_s[...]


def _ffn_kernel(layer, x_ref, w_up_hbm, cw_ref, w_down_hbm, pre_g_ref, post_g_ref, o_ref,
                w_up_ref, w_down_ref, up_stage, down_stage, w_sem, ubuf):
    t = pl.program_id(1)

    @pl.when(_first_grid_step())
    def _():
        _load_weight_bf16(w_up_hbm, layer, w_up_ref, up_stage, w_sem, *up_stage.shape)
        _load_weight_bf16(w_down_hbm, layer, w_down_ref, down_stage, w_sem, *down_stage.shape)

    @pl.when(t == 0)
    def _():
        ubuf[:, pl.ds(TQ_FFN, HALO), :] = jnp.zeros((2 * D_FF // LANES, HALO, LANES), F32)

    ubuf[:, 0:HALO, :] = ubuf[:, TQ_FFN:TQ_FFN + HALO, :]
    per_block = FF_BLOCK // LANES
    n_blocks = D_FF // FF_BLOCK

    for sub in range(TQ_FFN // SUB_TILE):
        r0 = SUB_TILE * sub
        x = x_ref[r0:r0 + SUB_TILE, :]
        h = _rmsnorm(x, pre_g_ref[...]).astype(BF16)

        def conv(slab):
            w = cw_ref[:, LANES * slab:LANES * (slab + 1)]
            lo = HALO + r0
            return (w[2:3] * ubuf[slab, lo:lo + SUB_TILE, :]
                    + w[1:2] * ubuf[slab, lo - 1:lo - 1 + SUB_TILE, :]
                    + w[0:1] * ubuf[slab, lo - 2:lo - 2 + SUB_TILE, :])

        def up_project(g):
            for base in (FF_BLOCK * g, D_FF + FF_BLOCK * g):
                u = _dot(h, w_up_ref[:, base:base + FF_BLOCK])
                for j in range(per_block):
                    ubuf[base // LANES + j, HALO + r0:HALO + r0 + SUB_TILE, :] = (
                        u[:, LANES * j:LANES * (j + 1)])

        def activate(g):
            act = []
            for j in range(per_block):
                gate = conv(FF_BLOCK * g // LANES + j)
                val = conv((D_FF + FF_BLOCK * g) // LANES + j)
                act.append((gate * _sigmoid(gate) * val).astype(BF16))
            return jnp.concatenate(act, axis=1)

        def down_project(y, g, act):
            part = _dot(act, w_down_ref[FF_BLOCK * g:FF_BLOCK * (g + 1), :])
            return part if y is None else y + part

        y = None
        prev_act = None
        up_project(0)
        for g in range(n_blocks):
            if g + 1 < n_blocks:
                up_project(g + 1)
            if g >= 1:
                y = down_project(y, g - 1, prev_act)
            prev_act = activate(g)
        y = down_project(y, n_blocks - 1, prev_act)

        o_ref[r0:r0 + SUB_TILE, :] = x + _rmsnorm(y, post_g_ref[...])


def _x_spec(tq):
    return pl.BlockSpec((None, tq, D_MODEL), lambda b, t: (b, t, 0))


def _layer_spec(shape, layer):
    return pl.BlockSpec((None,) + shape, lambda b, t: (layer,) + (0,) * len(shape),
                        pipeline_mode=pl.Buffered(1))


_HBM_SPEC = pl.BlockSpec(memory_space=pl.ANY)


_COMPILER_PARAMS = pltpu.CompilerParams(
    dimension_semantics=("arbitrary", "arbitrary"), vmem_limit_bytes=VMEM_LIMIT_BYTES)


def _mixer_call(layer, x, lb_param, w_in, w_out, conv_w, hgrn_g, pre_g, post_g):
    batch, seq, _ = x.shape
    return pl.pallas_call(
        functools.partial(_mixer_kernel, layer),
        out_shape=jax.ShapeDtypeStruct(x.shape, x.dtype),
        grid=(batch, seq // TQ),
        in_specs=[
            _x_spec(TQ),
            pl.BlockSpec((DEPTH, HGRN_WIDTH), lambda b, t: (0, 0)),
            _HBM_SPEC,
            _HBM_SPEC,
            _layer_spec((3, CONV_WIDTH), layer),
            _layer_spec((1, HGRN_WIDTH), layer),
            _layer_spec((1, D_MODEL), layer),
            _layer_spec((1, D_MODEL), layer),
        ],
        out_specs=_x_spec(TQ),
        scratch_shapes=[
            pltpu.VMEM((D_MODEL, IN_COLS), BF16),
            pltpu.VMEM((D_MODEL, D_MODEL), BF16),
            pltpu.VMEM(W_STAGE, F32),
            pltpu.SemaphoreType.DMA((1,)),
            pltpu.VMEM((TQ, D_MODEL), BF16),
            pltpu.VMEM((TQ, IN_COLS), F32),
            pltpu.VMEM((TQ, HGRN_WIDTH), F32),
            pltpu.VMEM((TQ, HGRN_WIDTH), F32),
            pltpu.VMEM((TQ, HGRN_WIDTH), F32),
            pltpu.VMEM((TQ, HGRN_WIDTH), F32),
            pltpu.VMEM((TQ, D_MODEL), BF16),
            pltpu.VMEM((CONV_WIDTH // LANES, TQ + HALO, LANES), F32),
            pltpu.VMEM((HEADS, HEAD_DIM, HEAD_DIM), F32),
            pltpu.VMEM((HEADS, HEAD_DIM, HEAD_DIM), F32),
            pltpu.VMEM((HEADS, CHUNK, HEAD_DIM), F32),
            pltpu.VMEM((HEADS, CHUNK, HEAD_DIM), F32),
            pltpu.VMEM((HEADS, CHUNK, HEAD_DIM), F32),
        ],
        compiler_params=_COMPILER_PARAMS,
        name=f"mixer_l{layer}",
    )(x, lb_param, w_in, w_out, conv_w, hgrn_g, pre_g, post_g)


def _ffn_call(layer, x, w_up, conv_w, w_down, pre_g, post_g):
    batch, seq, _ = x.shape
    return pl.pallas_call(
        functools.partial(_ffn_kernel, layer),
        out_shape=jax.ShapeDtypeStruct(x.shape, x.dtype),
        grid=(batch, seq // TQ_FFN),
        in_specs=[
            _x_spec(TQ_FFN),
            _HBM_SPEC,
            _layer_spec((3, 2 * D_FF), layer),
            _HBM_SPEC,
            _layer_spec((1, D_MODEL), layer),
            _layer_spec((1, D_MODEL), layer),
        ],
        out_specs=_x_spec(TQ_FFN),
        scratch_shapes=[
            pltpu.VMEM((D_MODEL, 2 * D_FF), BF16),
            pltpu.VMEM((D_FF, D_MODEL), BF16),
            pltpu.VMEM(W_STAGE, F32),
            pltpu.VMEM(W_STAGE_DOWN, F32),
            pltpu.SemaphoreType.DMA((1,)),
            pltpu.VMEM((2 * D_FF // LANES, TQ_FFN + HALO, LANES), F32),
        ],
        compiler_params=_COMPILER_PARAMS,
        name=f"ffn_l{layer}",
    )(x, w_up, conv_w, w_down, pre_g, post_g)


def kernel(x, lb_param, w_in, w_out, conv_w, ffn_w_up, ffn_conv_w, ffn_w_down, hgrn_norm_g,
           pre_mix_g, post_mix_g, pre_ffn_g, post_ffn_g):
    assert x.shape[1] % TQ == 0 and x.shape[1] % TQ_FFN == 0 and x.shape[2] == D_MODEL
    hgrn_g = hgrn_norm_g[:, None, :]
    pre_mix, post_mix = pre_mix_g[:, None, :], post_mix_g[:, None, :]
    pre_ffn, post_ffn = pre_ffn_g[:, None, :], post_ffn_g[:, None, :]
    for layer in range(DEPTH):
        x = _mixer_call(layer, x, lb_param, w_in, w_out, conv_w, hgrn_g, pre_mix, post_mix)
        x = _ffn_call(layer, x, ffn_w_up, ffn_conv_w, ffn_w_down, pre_ffn, post_ffn)
    return x
```

```python
import functools

import jax
import jax.numpy as jnp
from jax import lax
from jax.experimental import pallas as pl
from jax.experimental.pallas import tpu as pltpu

D_MODEL = 1024
DEPTH = 4
HGRN_WIDTH = 512
HEADS = 4
HEAD_DIM = 128
CONV_WIDTH = 512
IN_COLS = 4 * HGRN_WIDTH + 3 * CONV_WIDTH
D_FF = 2816
EPS = 1e-6

TQ = 512
TQ_FFN = 512
SUB_TILE = 256
FCHUNK = 128
MAX_LOG_SPAN = 80.0
CHUNK = 64
SUB = 16
NSUB = CHUNK // SUB
HALO = 8
LANES = 128
FF_BLOCK = 256
NEG_BIG = -1e30
W_SLOTS = 3
W_STAGE = (1024, 512)
W_STAGE_DOWN = (256, 1024)
VMEM_LIMIT_BYTES = 56 * 1024 * 1024

F32 = jnp.float32
BF16 = jnp.bfloat16


def _dot(a, b):
    return jnp.dot(a, b, preferred_element_type=F32)


def _dot_nt(a, b):
    return lax.dot_general(a, b, (((1,), (1,)), ((), ())), preferred_element_type=F32)


def _dot_tn(a, b):
    return lax.dot_general(a, b, (((0,), (0,)), ((), ())), preferred_element_type=F32)


def _rmsnorm(x, g):
    ms = jnp.mean(x * x, axis=-1, keepdims=True)
    return x * lax.rsqrt(ms + EPS) * g


def _sigmoid(x):
    return 1.0 / (1.0 + jnp.exp(-x))


def _first_grid_step():
    return jnp.logical_and(pl.program_id(0) == 0, pl.program_id(1) == 0)


def _load_weight_bf16(w_hbm, layer, w_bf, stage, sem, row_chunk, col_chunk):
    rows, cols = w_bf.shape
    n_slots = stage.shape[0]
    pieces = [(row_chunk * r, col_chunk * c)
              for r in range(rows // row_chunk) for c in range(cols // col_chunk)]

    def copy(i):
        r0, c0 = pieces[i]
        src = w_hbm.at[layer, pl.ds(r0, row_chunk), pl.ds(c0, col_chunk)]
        return pltpu.make_async_copy(src, stage.at[i % n_slots], sem.at[i % n_slots])

    for i in range(min(n_slots, len(pieces))):
        copy(i).start()
    for i, (r0, c0) in enumerate(pieces):
        copy(i).wait()
        w_bf[r0:r0 + row_chunk, c0:c0 + col_chunk] = stage[i % n_slots].astype(BF16)
        if i + n_slots < len(pieces):
            copy(i + n_slots).start()


def _layer_lower_bound(lb_ref, layer):
    lbp = lb_ref[...]
    m = jnp.max(lbp, axis=0, keepdims=True)
    e = jnp.exp(lbp - m)
    p = e / jnp.sum(e, axis=0, keepdims=True)
    lb = p[1:2]
    for j in range(2, layer + 1):
        lb = lb + p[j:j + 1]
    return lb


def _bcast_row(ref, head, row, nrows):
    return jnp.broadcast_to(ref[head, pl.ds(row, 1), :], (nrows, HEAD_DIM))


def _hgrn_head_offdiag(head, q, k, v, b, st_ref, b_s):
    blk = [slice(SUB * i, SUB * (i + 1)) for i in range(NSUB)]
    bend = [_bcast_row(b_s, head, SUB * i + SUB - 1, SUB) for i in range(NSUB)]
    qp = [q[blk[0]] * jnp.exp(b[blk[0]])]
    for i in range(1, NSUB):
        qp.append(q[blk[i]] * jnp.exp(b[blk[i]] - bend[i - 1]))
    kp = [k[blk[i]] * jnp.exp(bend[i] - b[blk[i]]) for i in range(NSUB)]

    st = st_ref[head]
    st_b = st.astype(BF16)
    qb = [qp[0]] + [qp[i] * jnp.exp(bend[i - 1]) for i in range(1, NSUB)]
    o_inter = _dot_nt(jnp.concatenate(qb, axis=0).astype(BF16), st_b)
    out = [o_inter[blk[i]] for i in range(NSUB)]

    v_b = v.astype(BF16)
    for i in range(1, NSUB):
        parts = [kp[j] * jnp.exp(bend[i - 1] - bend[j]) for j in range(i - 1)] + [kp[i - 1]]
        kcat = jnp.concatenate(parts, axis=0) if len(parts) > 1 else parts[0]
        a = _dot_nt(qp[i].astype(BF16), kcat.astype(BF16))
        out[i] = out[i] + _dot(a.astype(BF16), v_b[0:SUB * i])

    last = NSUB - 1
    kend = jnp.concatenate(
        [kp[j] * jnp.exp(bend[last] - bend[j]) for j in range(last)] + [kp[last]], axis=0)
    decay = jnp.exp(_bcast_row(b_s, head, CHUNK - 1, HEAD_DIM))
    st_ref[head] = st * decay + _dot_tn(v_b, kend.astype(BF16))
    return out


def _hgrn_pair_diag(heads, qs, bs, out, b_s, k_s, v_s, ones_bd):
    row = lax.broadcasted_iota(jnp.int32, (8, HEAD_DIM), 0)
    for i in range(NSUB):
        pieces = {h: [] for h in heads}
        for h, q, b in zip(heads, qs, bs):
            q_lo, q_hi = q[SUB * i:SUB * i + 8], q[SUB * i + 8:SUB * (i + 1)]
            b_lo, b_hi = b[SUB * i:SUB * i + 8], b[SUB * i + 8:SUB * (i + 1)]
            for s in range(SUB):
                r = SUB * i + s
                b_row = _bcast_row(b_s, h, r, 8)
                k_row = _bcast_row(k_s, h, r, 8)
                if s < 8:
                    d_lo = jnp.where(row >= s, b_lo - b_row, NEG_BIG)
                    pieces[h].append(q_lo * jnp.exp(d_lo) * k_row)
                    pieces[h].append(q_hi * jnp.exp(b_hi - b_row) * k_row)
                else:
                    d_hi = jnp.where(row >= s - 8, b_hi - b_row, NEG_BIG)
                    pieces[h].append(q_hi * jnp.exp(d_hi) * k_row)
        p = jnp.concatenate(
            [jnp.concatenate(pieces[h], axis=0) for h in heads], axis=1).astype(BF16)
        red = _dot(p, ones_bd)
        for hi, h in enumerate(heads):
            lanes = slice(HEAD_DIM * hi, HEAD_DIM * (hi + 1))
            acc_lo = out[h][i][0:8]
            acc_hi = out[h][i][8:SUB]
            n = 0
            for s in range(SUB):
                v_row = _bcast_row(v_s, h, SUB * i + s, 8)
                if s < 8:
                    acc_lo = acc_lo + red[8 * n:8 * n + 8, lanes] * v_row
                    n += 1
                acc_hi = acc_hi + red[8 * n:8 * n + 8, lanes] * v_row
                n += 1
            out[h][i] = jnp.concatenate([acc_lo, acc_hi], axis=0)


def _decay_consts(layer, lb_ref):
    if layer == 0:
        return None
    lb = _layer_lower_bound(lb_ref, layer)
    return jnp.log(lb), jnp.log1p(-lb), 1.0 - lb


def _gate_prep(consts, r0, z_s, q_s, k_s, g_s, b_s, tri):
    rows = slice(r0, r0 + SUB_TILE)
    for hd in range(HEADS):
        lanes = slice(HEAD_DIM * hd, HEAD_DIM * (hd + 1))
        fp = z_s[rows, HGRN_WIDTH + HEAD_DIM * hd:HGRN_WIDTH + HEAD_DIM * (hd + 1)]
        e = jnp.exp(-jnp.abs(fp))
        log_sig = jnp.minimum(fp, 0.0) - jnp.log(1.0 + e)
        r = 1.0 / (1.0 + e)
        sig_neg = jnp.where(fp >= 0.0, e * r, r)
        if consts is None:
            log_f, kk = log_sig, sig_neg
        else:
            log_lb, log_1m, one_m = consts
            cc = log_1m[:, lanes] + log_sig
            la = log_lb[:, lanes]
            log_f = jnp.maximum(la, cc) + jnp.log(1.0 + jnp.exp(-jnp.abs(la - cc)))
            kk = one_m[:, lanes] * sig_neg
        k_s[rows, lanes] = kk
        b_s[rows, lanes] = log_f
        qz = z_s[rows, lanes]
        q_s[rows, lanes] = qz * _sigmoid(qz) * (HEAD_DIM ** -0.5)
        gz = z_s[rows, 3 * HGRN_WIDTH + HEAD_DIM * hd:3 * HGRN_WIDTH + HEAD_DIM * (hd + 1)]
        g_s[rows, lanes] = gz * _sigmoid(gz)

    span = None
    for c in range(SUB_TILE // FCHUNK):
        crows = slice(r0 + FCHUNK * c, r0 + FCHUNK * (c + 1))
        log_f = b_s[crows, :]
        g1 = log_f.astype(BF16)
        r1 = log_f - g1.astype(F32)
        g2 = r1.astype(BF16)
        g3 = (r1 - g2.astype(F32)).astype(BF16)
        b = _dot(tri, g1) + _dot(tri, g2) + _dot(tri, g3)
        b_s[crows, :] = b
        mid = b[FCHUNK // 2 - 1:FCHUNK // 2]
        s = jnp.maximum(b[0:1] - mid, mid - b[FCHUNK - 1:FCHUNK])
        span = s if span is None else jnp.maximum(span, s)
    return span


def _hgrn_scores(c0, hd, z_s, q_s, k_s, b_s):
    lanes = slice(HEAD_DIM * hd, HEAD_DIM * (hd + 1))
    rows = slice(c0, c0 + FCHUNK)
    b = b_s[rows, lanes]
    mid = b_s[c0 + FCHUNK // 2 - 1:c0 + FCHUNK // 2, lanes]
    last = b_s[c0 + FCHUNK - 1:c0 + FCHUNK, lanes]
    v = z_s[rows, 2 * HGRN_WIDTH + HEAD_DIM * hd:2 * HGRN_WIDTH + HEAD_DIM * (hd + 1)]
    v = v.astype(BF16)
    qs = q_s[rows, lanes] * jnp.exp(b - mid)
    ks = k_s[rows, lanes] * jnp.exp(mid - b)
    a = _dot_nt(qs.astype(BF16), ks.astype(BF16))
    qe = (qs * jnp.exp(mid)).astype(BF16)
    ke = (ks * jnp.exp(last - mid)).astype(BF16)
    return a, qe, _dot_tn(v, ke), jnp.exp(last)


def _hgrn_outputs(c0, hd, st, scores, z_s, g_s, hg, mix_s, tril):
    a, qe, kv, decay = scores
    lanes = slice(HEAD_DIM * hd, HEAD_DIM * (hd + 1))
    rows = slice(c0, c0 + FCHUNK)
    v = z_s[rows, 2 * HGRN_WIDTH + HEAD_DIM * hd:2 * HGRN_WIDTH + HEAD_DIM * (hd + 1)]
    v = v.astype(BF16)
    a = jnp.where(tril, a, 0.0).astype(BF16)
    o = _dot(a, v) + _dot_nt(qe, st.astype(BF16))
    st = st * decay + kv
    ms = jnp.mean(o * o, axis=-1, keepdims=True)
    on = o * lax.rsqrt(ms + EPS) * hg[:, lanes] * g_s[rows, lanes]
    mix_s[rows, lanes] = on.astype(BF16)
    return st


def _hgrn_any_range(z_s, q_s, k_s, g_s, b_s, hg, st_s, mix_s, cb_s, ck_s, cv_s):
    ones_bd = jnp.where(
        lax.broadcasted_iota(jnp.int32, (2 * HEAD_DIM, 2 * HEAD_DIM), 0) // HEAD_DIM
        == lax.broadcasted_iota(jnp.int32, (2 * HEAD_DIM, 2 * HEAD_DIM), 1) // HEAD_DIM,
        1.0, 0.0).astype(BF16)

    def chunk_body(c, carry):
        for half in range(FCHUNK // CHUNK):
            r0 = pl.multiple_of(c * FCHUNK + CHUNK * half, CHUNK)
            rows = pl.ds(r0, CHUNK)
            b = b_s[rows, :]
            if half:
                b = b - b_s[pl.ds(r0 - 1, 1), :]
            q = q_s[rows, :]
            kk = k_s[rows, :]
            v = z_s[rows, 2 * HGRN_WIDTH:3 * HGRN_WIDTH]
            gate = g_s[rows, :]
            cb_s[...] = jnp.stack([b[:, HEAD_DIM * hd:HEAD_DIM * (hd + 1)] for hd in range(HEADS)])
            ck_s[...] = jnp.stack([kk[:, HEAD_DIM * hd:HEAD_DIM * (hd + 1)] for hd in range(HEADS)])
            cv_s[...] = jnp.stack([v[:, HEAD_DIM * hd:HEAD_DIM * (hd + 1)] for hd in range(HEADS)])
            out = {}
            for hd in range(HEADS):
                lanes = slice(HEAD_DIM * hd, HEAD_DIM * (hd + 1))
                out[hd] = _hgrn_head_offdiag(hd, q[:, lanes], kk[:, lanes], v[:, lanes],
                                             b[:, lanes], st_s, cb_s)
            for pair in ((0, 1), (2, 3)):
                _hgrn_pair_diag(pair,
                                [q[:, HEAD_DIM * hd:HEAD_DIM * (hd + 1)] for hd in pair],
                                [b[:, HEAD_DIM * hd:HEAD_DIM * (hd + 1)] for hd in pair],
                                out, cb_s, ck_s, cv_s, ones_bd)
            for hd in range(HEADS):
                lanes = slice(HEAD_DIM * hd, HEAD_DIM * (hd + 1))
                oh = jnp.concatenate(out[hd], axis=0)
                ms = jnp.mean(oh * oh, axis=-1, keepdims=True)
                on = oh * lax.rsqrt(ms + EPS) * hg[:, lanes] * gate[:, lanes]
                mix_s[rows, lanes] = on.astype(BF16)
        return carry

    lax.fori_loop(0, TQ // FCHUNK, chunk_body, 0)


GATE_GROUPS = (1, 0, 3)
OTHER_GROUPS = (2, 5, 6, 4)


def _mixer_kernel(layer, x_ref, lb_ref, w_in_hbm, w_out_hbm, cw_ref, hg_ref, pre_g_ref,
                  post_g_ref, o_ref, w_in_ref, w_out_ref, w_stage, w_sem, h_s, z_s, q_s, k_s, g_s,
                  b_s, mix_s, cbuf, st_s, st_next_s, cb_s, ck_s, cv_s):
    t = pl.program_id(1)
    subs = [SUB_TILE * i for i in range(TQ // SUB_TILE)]

    @pl.when(_first_grid_step())
    def _():
        _load_weight_bf16(w_in_hbm, layer, w_in_ref, w_stage, w_sem, *w_stage.shape[1:])
        _load_weight_bf16(w_out_hbm, layer, w_out_ref, w_stage, w_sem, *w_stage.shape[1:])

    @pl.when(t == 0)
    def _():
        st_s[...] = jnp.zeros_like(st_s)
        cbuf[:, pl.ds(TQ, HALO), :] = jnp.zeros((CONV_WIDTH // LANES, HALO, LANES), F32)

    cbuf[:, 0:HALO, :] = cbuf[:, TQ:TQ + HALO, :]

    def in_project(r0):
        rows = slice(r0, r0 + SUB_TILE)
        h_s[rows, :] = _rmsnorm(x_ref[rows, :], pre_g_ref[...]).astype(BF16)
        for g in GATE_GROUPS + OTHER_GROUPS:
            cols = slice(512 * g, 512 * (g + 1))
            z_s[rows, cols] = _dot(h_s[rows, :], w_in_ref[:, cols])

    consts = _decay_consts(layer, lb_ref)
    tril = (lax.broadcasted_iota(jnp.int32, (FCHUNK, FCHUNK), 0)
            >= lax.broadcasted_iota(jnp.int32, (FCHUNK, FCHUNK), 1))
    tri = jnp.where(tril, 1.0, 0.0).astype(BF16)
    hg = hg_ref[...]
    cw = cw_ref[...]

    def out_project(r0):
        rows = slice(r0, r0 + SUB_TILE)
        mix = _dot(mix_s[rows, :], w_out_ref[...])
        o_ref[rows, :] = x_ref[rows, :] + _rmsnorm(mix, post_g_ref[...])

    state = [st_s[hd] for hd in range(HEADS)]
    span = None
    in_project(subs[0])
    for i, r0 in enumerate(subs):
        rows = slice(r0, r0 + SUB_TILE)
        s = _gate_prep(consts, r0, z_s, q_s, k_s, g_s, b_s, tri)
        span = s if span is None else jnp.maximum(span, s)
        chunk_rows = [r0 + FCHUNK * c for c in range(SUB_TILE // FCHUNK)]
        scores = {(hd, c0): _hgrn_scores(c0, hd, z_s, q_s, k_s, b_s)
                  for hd in range(HEADS) for c0 in chunk_rows}
        for hd in range(HEADS):
            for c0 in chunk_rows:
                state[hd] = _hgrn_outputs(c0, hd, state[hd], scores[hd, c0], z_s, g_s, hg,
                                          mix_s, tril)
        lo = HALO + r0
        for j in range(CONV_WIDTH // LANES):
            lanes = slice(LANES * j, LANES * (j + 1))
            cbuf[j, lo:lo + SUB_TILE, :] = (z_s[rows, 2560 + LANES * j:2560 + LANES * (j + 1)]
                                            * z_s[rows, 3072 + LANES * j:3072 + LANES * (j + 1)])
            y = (cw[2:3, lanes] * cbuf[j, lo:lo + SUB_TILE, :]
                 + cw[1:2, lanes] * cbuf[j, lo - 1:lo - 1 + SUB_TILE, :]
                 + cw[0:1, lanes] * cbuf[j, lo - 2:lo - 2 + SUB_TILE, :])
            gate_b = z_s[rows, 2048 + LANES * j:2048 + LANES * (j + 1)]
            mix_s[rows, HGRN_WIDTH + LANES * j:HGRN_WIDTH + LANES * (j + 1)] = (
                gate_b * y).astype(BF16)
        if i + 1 < len(subs):
            in_project(subs[i + 1])
        if i >= 1:
            out_project(subs[i - 1])
    out_project(subs[-1])
    for hd in range(HEADS):
        st_next_s[hd] = state[hd]

    @pl.when(jnp.logical_not(jnp.max(span) <= MAX_LOG_SPAN))
    def _():
        st_next_s[...] = st_s[...]
        _hgrn_any_range(z_s, q_s, k_s, g_s, b_s, hg, st_next_s, mix_s, cb_s, ck_s, cv_s)
        for r0 in subs:
            out_project(r0)

    st_s[...] = st_next_s[...]


def _ffn_kernel(layer, x_ref, w_up_hbm, cw_ref, w_down_hbm, pre_g_ref, post_g_ref, o_ref,
                w_up_ref, w_down_ref, up_stage, down_stage, w_sem, ubuf):
    t = pl.program_id(1)

    @pl.when(_first_grid_step())
    def _():
        _load_weight_bf16(w_up_hbm, layer, w_up_ref, up_stage, w_sem, *up_stage.shape[1:])
        _load_weight_bf16(w_down_hbm, layer, w_down_ref, down_stage, w_sem, *down_stage.shape[1:])

    @pl.when(t == 0)
    def _():
        ubuf[:, pl.ds(TQ_FFN, HALO), :] = jnp.zeros((2 * D_FF // LANES, HALO, LANES), F32)

    ubuf[:, 0:HALO, :] = ubuf[:, TQ_FFN:TQ_FFN + HALO, :]
    per_block = FF_BLOCK // LANES
    n_blocks = D_FF // FF_BLOCK

    for sub in range(TQ_FFN // SUB_TILE):
        r0 = SUB_TILE * sub
        x = x_ref[r0:r0 + SUB_TILE, :]
        h = _rmsnorm(x, pre_g_ref[...]).astype(BF16)

        def conv(slab):
            w = cw_ref[:, LANES * slab:LANES * (slab + 1)]
            lo = HALO + r0
            return (w[2:3] * ubuf[slab, lo:lo + SUB_TILE, :]
                    + w[1:2] * ubuf[slab, lo - 1:lo - 1 + SUB_TILE, :]
                    + w[0:1] * ubuf[slab, lo - 2:lo - 2 + SUB_TILE, :])

        def up_project(g):
            for base in (FF_BLOCK * g, D_FF + FF_BLOCK * g):
                u = _dot(h, w_up_ref[:, base:base + FF_BLOCK])
                for j in range(per_block):
                    ubuf[base // LANES + j, HALO + r0:HALO + r0 + SUB_TILE, :] = (
                        u[:, LANES * j:LANES * (j + 1)])

        def activate(g):
            act = []
            for j in range(per_block):
                gate = conv(FF_BLOCK * g // LANES + j)
                val = conv((D_FF + FF_BLOCK * g) // LANES + j)
                act.append((gate * _sigmoid(gate) * val).astype(BF16))
            return jnp.concatenate(act, axis=1)

        def down_project(y, g, act):
            part = _dot(act, w_down_ref[FF_BLOCK * g:FF_BLOCK * (g + 1), :])
            return part if y is None else y + part

        y = None
        prev_act = None
        up_project(0)
        for g in range(n_blocks):
            if g + 1 < n_blocks:
                up_project(g + 1)
            if g >= 1:
                y = down_project(y, g - 1, prev_act)
            prev_act = activate(g)
        y = down_project(y, n_blocks - 1, prev_act)

        o_ref[r0:r0 + SUB_TILE, :] = x + _rmsnorm(y, post_g_ref[...])


def _x_spec(tq):
    return pl.BlockSpec((None, tq, D_MODEL), lambda b, t: (b, t, 0))


def _layer_spec(shape, layer):
    return pl.BlockSpec((None,) + shape, lambda b, t: (layer,) + (0,) * len(shape),
                        pipeline_mode=pl.Buffered(1))


_HBM_SPEC = pl.BlockSpec(memory_space=pl.ANY)


_COMPILER_PARAMS = pltpu.CompilerParams(
    dimension_semantics=("arbitrary", "arbitrary"), vmem_limit_bytes=VMEM_LIMIT_BYTES)


def _mixer_call(layer, x, lb_param, w_in, w_out, conv_w, hgrn_g, pre_g, post_g):
    batch, seq, _ = x.shape
    return pl.pallas_call(
        functools.partial(_mixer_kernel, layer),
        out_shape=jax.ShapeDtypeStruct(x.shape, x.dtype),
        grid=(batch, seq // TQ),
        in_specs=[
            _x_spec(TQ),
            pl.BlockSpec((DEPTH, HGRN_WIDTH), lambda b, t: (0, 0)),
            _HBM_SPEC,
            _HBM_SPEC,
            _layer_spec((3, CONV_WIDTH), layer),
            _layer_spec((1, HGRN_WIDTH), layer),
            _layer_spec((1, D_MODEL), layer),
            _layer_spec((1, D_MODEL), layer),
        ],
        out_specs=_x_spec(TQ),
        scratch_shapes=[
            pltpu.VMEM((D_MODEL, IN_COLS), BF16),
            pltpu.VMEM((D_MODEL, D_MODEL), BF16),
            pltpu.VMEM((W_SLOTS,) + W_STAGE, F32),
            pltpu.SemaphoreType.DMA((W_SLOTS,)),
            pltpu.VMEM((TQ, D_MODEL), BF16),
            pltpu.VMEM((TQ, IN_COLS), F32),
            pltpu.VMEM((TQ, HGRN_WIDTH), F32),
            pltpu.VMEM((TQ, HGRN_WIDTH), F32),
            pltpu.VMEM((TQ, HGRN_WIDTH), F32),
            pltpu.VMEM((TQ, HGRN_WIDTH), F32),
            pltpu.VMEM((TQ, D_MODEL), BF16),
            pltpu.VMEM((CONV_WIDTH // LANES, TQ + HALO, LANES), F32),
            pltpu.VMEM((HEADS, HEAD_DIM, HEAD_DIM), F32),
            pltpu.VMEM((HEADS, HEAD_DIM, HEAD_DIM), F32),
            pltpu.VMEM((HEADS, CHUNK, HEAD_DIM), F32),
            pltpu.VMEM((HEADS, CHUNK, HEAD_DIM), F32),
            pltpu.VMEM((HEADS, CHUNK, HEAD_DIM), F32),
        ],
        compiler_params=_COMPILER_PARAMS,
        name=f"mixer_l{layer}",
    )(x, lb_param, w_in, w_out, conv_w, hgrn_g, pre_g, post_g)


def _ffn_call(layer, x, w_up, conv_w, w_down, pre_g, post_g):
    batch, seq, _ = x.shape
    return pl.pallas_call(
        functools.partial(_ffn_kernel, layer),
        out_shape=jax.ShapeDtypeStruct(x.shape, x.dtype),
        grid=(batch, seq // TQ_FFN),
        in_specs=[
            _x_spec(TQ_FFN),
            _HBM_SPEC,
            _layer_spec((3, 2 * D_FF), layer),
            _HBM_SPEC,
            _layer_spec((1, D_MODEL), layer),
            _layer_spec((1, D_MODEL), layer),
        ],
        out_specs=_x_spec(TQ_FFN),
        scratch_shapes=[
            pltpu.VMEM((D_MODEL, 2 * D_FF), BF16),
            pltpu.VMEM((D_FF, D_MODEL), BF16),
            pltpu.VMEM((W_SLOTS,) + W_STAGE, F32),
            pltpu.VMEM((W_SLOTS,) + W_STAGE_DOWN, F32),
            pltpu.SemaphoreType.DMA((W_SLOTS,)),
            pltpu.VMEM((2 * D_FF // LANES, TQ_FFN + HALO, LANES), F32),
        ],
        compiler_params=_COMPILER_PARAMS,
        name=f"ffn_l{layer}",
    )(x, w_up, conv_w, w_down, pre_g, post_g)


def kernel(x, lb_param, w_in, w_out, conv_w, ffn_w_up, ffn_conv_w, ffn_w_down, hgrn_norm_g,
           pre_mix_g, post_mix_g, pre_ffn_g, post_ffn_g):
    assert x.shape[1] % TQ == 0 and x.shape[1] % TQ_FFN == 0 and x.shape[2] == D_MODEL
    hgrn_g = hgrn_norm_g[:, None, :]
    pre_mix, post_mix = pre_mix_g[:, None, :], post_mix_g[:, None, :]
    pre_ffn, post_ffn = pre_ffn_g[:, None, :], post_ffn_g[:, None, :]
    for layer in range(DEPTH):
        x = _mixer_call(layer, x, lb_param, w_in, w_out, conv_w, hgrn_g, pre_mix, post_mix)
        x = _ffn_call(layer, x, ffn_w_up, ffn_conv_w, ffn_w_down, pre_ffn, post_ffn)
    return x
```

```python
import functools

import jax
import jax.numpy as jnp
from jax import lax
from jax.experimental import pallas as pl
from jax.experimental.pallas import tpu as pltpu

D_MODEL = 1024
DEPTH = 4
HGRN_WIDTH = 512
HEADS = 4
HEAD_DIM = 128
CONV_WIDTH = 512
IN_COLS = 4 * HGRN_WIDTH + 3 * CONV_WIDTH
D_FF = 2816
EPS = 1e-6

TQ = 512
TQ_FFN = 512
SUB_TILE = 256
FCHUNK = 128
MAX_LOG_SPAN = 80.0
CHUNK = 64
SUB = 16
NSUB = CHUNK // SUB
HALO = 8
LANES = 128
FF_BLOCK = 256
NEG_BIG = -1e30
W_SLOTS = 3
W_STAGE = (1024, 512)
W_STAGE_DOWN = (256, 1024)
VMEM_LIMIT_BYTES = 56 * 1024 * 1024

F32 = jnp.float32
BF16 = jnp.bfloat16


def _dot(a, b):
    return jnp.dot(a, b, preferred_element_type=F32)


def _dot_nt(a, b):
    return lax.dot_general(a, b, (((1,), (1,)), ((), ())), preferred_element_type=F32)


def _dot_tn(a, b):
    return lax.dot_general(a, b, (((0,), (0,)), ((), ())), preferred_element_type=F32)


def _rmsnorm(x, g):
    ms = jnp.mean(x * x, axis=-1, keepdims=True)
    return x * lax.rsqrt(ms + EPS) * g


def _sigmoid(x):
    return 1.0 / (1.0 + jnp.exp(-x))


def _first_grid_step():
    return jnp.logical_and(pl.program_id(0) == 0, pl.program_id(1) == 0)


def _load_weight_bf16(w_hbm, layer, w_bf, stage, sem, row_chunk, col_chunk):
    rows, cols = w_bf.shape
    n_slots = stage.shape[0]
    pieces = [(row_chunk * r, col_chunk * c)
              for r in range(rows // row_chunk) for c in range(cols // col_chunk)]

    def copy(i):
        r0, c0 = pieces[i]
        src = w_hbm.at[layer, pl.ds(r0, row_chunk), pl.ds(c0, col_chunk)]
        return pltpu.make_async_copy(src, stage.at[i % n_slots], sem.at[i % n_slots])

    for i in range(min(n_slots, len(pieces))):
        copy(i).start()
    for i, (r0, c0) in enumerate(pieces):
        copy(i).wait()
        w_bf[r0:r0 + row_chunk, c0:c0 + col_chunk] = stage[i % n_slots].astype(BF16)
        if i + n_slots < len(pieces):
            copy(i + n_slots).start()


def _layer_lower_bound(lb_ref, layer):
    lbp = lb_ref[...]
    m = jnp.max(lbp, axis=0, keepdims=True)
    e = jnp.exp(lbp - m)
    p = e / jnp.sum(e, axis=0, keepdims=True)
    lb = p[1:2]
    for j in range(2, layer + 1):
        lb = lb + p[j:j + 1]
    return lb


def _bcast_row(ref, head, row, nrows):
    return jnp.broadcast_to(ref[head, pl.ds(row, 1), :], (nrows, HEAD_DIM))


def _hgrn_head_offdiag(head, q, k, v, b, st_ref, b_s):
    blk = [slice(SUB * i, SUB * (i + 1)) for i in range(NSUB)]
    bend = [_bcast_row(b_s, head, SUB * i + SUB - 1, SUB) for i in range(NSUB)]
    qp = [q[blk[0]] * jnp.exp(b[blk[0]])]
    for i in range(1, NSUB):
        qp.append(q[blk[i]] * jnp.exp(b[blk[i]] - bend[i - 1]))
    kp = [k[blk[i]] * jnp.exp(bend[i] - b[blk[i]]) for i in range(NSUB)]

    st = st_ref[head]
    st_b = st.astype(BF16)
    qb = [qp[0]] + [qp[i] * jnp.exp(bend[i - 1]) for i in range(1, NSUB)]
    o_inter = _dot_nt(jnp.concatenate(qb, axis=0).astype(BF16), st_b)
    out = [o_inter[blk[i]] for i in range(NSUB)]

    v_b = v.astype(BF16)
    for i in range(1, NSUB):
        parts = [kp[j] * jnp.exp(bend[i - 1] - bend[j]) for j in range(i - 1)] + [kp[i - 1]]
        kcat = jnp.concatenate(parts, axis=0) if len(parts) > 1 else parts[0]
        a = _dot_nt(qp[i].astype(BF16), kcat.astype(BF16))
        out[i] = out[i] + _dot(a.astype(BF16), v_b[0:SUB * i])

    last = NSUB - 1
    kend = jnp.concatenate(
        [kp[j] * jnp.exp(bend[last] - bend[j]) for j in range(last)] + [kp[last]], axis=0)
    decay = jnp.exp(_bcast_row(b_s, head, CHUNK - 1, HEAD_DIM))
    st_ref[head] = st * decay + _dot_tn(v_b, kend.astype(BF16))
    return out


def _hgrn_pair_diag(heads, qs, bs, out, b_s, k_s, v_s, ones_bd):
    row = lax.broadcasted_iota(jnp.int32, (8, HEAD_DIM), 0)
    for i in range(NSUB):
        pieces = {h: [] for h in heads}
        for h, q, b in zip(heads, qs, bs):
            q_lo, q_hi = q[SUB * i:SUB * i + 8], q[SUB * i + 8:SUB * (i + 1)]
            b_lo, b_hi = b[SUB * i:SUB * i + 8], b[SUB * i + 8:SUB * (i + 1)]
            for s in range(SUB):
                r = SUB * i + s
                b_row = _bcast_row(b_s, h, r, 8)
                k_row = _bcast_row(k_s, h, r, 8)
                if s < 8:
                    d_lo = jnp.where(row >= s, b_lo - b_row, NEG_BIG)
                    pieces[h].append(q_lo * jnp.exp(d_lo) * k_row)
                    pieces[h].append(q_hi * jnp.exp(b_hi - b_row) * k_row)
                else:
                    d_hi = jnp.where(row >= s - 8, b_hi - b_row, NEG_BIG)
                    pieces[h].append(q_hi * jnp.exp(d_hi) * k_row)
        p = jnp.concatenate(
            [jnp.concatenate(pieces[h], axis=0) for h in heads], axis=1).astype(BF16)
        red = _dot(p, ones_bd)
        for hi, h in enumerate(heads):
            lanes = slice(HEAD_DIM * hi, HEAD_DIM * (hi + 1))
            acc_lo = out[h][i][0:8]
            acc_hi = out[h][i][8:SUB]
            n = 0
            for s in range(SUB):
                v_row = _bcast_row(v_s, h, SUB * i + s, 8)
                if s < 8:
                    acc_lo = acc_lo + red[8 * n:8 * n + 8, lanes] * v_row
                    n += 1
                acc_hi = acc_hi + red[8 * n:8 * n + 8, lanes] * v_row
                n += 1
            out[h][i] = jnp.concatenate([acc_lo, acc_hi], axis=0)


def _decay_consts(layer, lb_ref):
    if layer == 0:
        return None
    lb = _layer_lower_bound(lb_ref, layer)
    return jnp.log(lb), jnp.log1p(-lb), 1.0 - lb


def _gate_prep(consts, r0, z_s, q_s, k_s, g_s, b_s, tri):
    rows = slice(r0, r0 + SUB_TILE)
    q_abs = []
    for hd in range(HEADS):
        lanes = slice(HEAD_DIM * hd, HEAD_DIM * (hd + 1))
        fp = z_s[rows, HGRN_WIDTH + HEAD_DIM * hd:HGRN_WIDTH + HEAD_DIM * (hd + 1)]
        e = jnp.exp(-jnp.abs(fp))
        log_sig = jnp.minimum(fp, 0.0) - jnp.log(1.0 + e)
        r = 1.0 / (1.0 + e)
        sig_neg = jnp.where(fp >= 0.0, e * r, r)
        if consts is None:
            log_f, kk = log_sig, sig_neg
        else:
            log_lb, log_1m, one_m = consts
            cc = log_1m[:, lanes] + log_sig
            la = log_lb[:, lanes]
            log_f = jnp.maximum(la, cc) + jnp.log(1.0 + jnp.exp(-jnp.abs(la - cc)))
            kk = one_m[:, lanes] * sig_neg
        k_s[rows, lanes] = kk
        b_s[rows, lanes] = log_f
        qz = z_s[rows, lanes]
        q = qz * _sigmoid(qz) * (HEAD_DIM ** -0.5)
        q_s[rows, lanes] = q
        q_abs.append(jnp.max(jnp.abs(q), axis=0, keepdims=True))
        gz = z_s[rows, 3 * HGRN_WIDTH + HEAD_DIM * hd:3 * HGRN_WIDTH + HEAD_DIM * (hd + 1)]
        g_s[rows, lanes] = gz * _sigmoid(gz)

    span = None
    for c in range(SUB_TILE // FCHUNK):
        crows = slice(r0 + FCHUNK * c, r0 + FCHUNK * (c + 1))
        log_f = b_s[crows, :]
        g1 = log_f.astype(BF16)
        r1 = log_f - g1.astype(F32)
        g2 = r1.astype(BF16)
        g3 = (r1 - g2.astype(F32)).astype(BF16)
        b = _dot(tri, g1) + _dot(tri, g2) + _dot(tri, g3)
        b_s[crows, :] = b
        mid = b[FCHUNK // 2 - 1:FCHUNK // 2]
        s = jnp.maximum(b[0:1] - mid, mid - b[FCHUNK - 1:FCHUNK])
        span = s if span is None else jnp.maximum(span, s)
    return span + jnp.maximum(jnp.log(jnp.concatenate(q_abs, axis=1)), 0.0)


def _hgrn_scores(c0, hd, z_s, q_s, k_s, b_s):
    lanes = slice(HEAD_DIM * hd, HEAD_DIM * (hd + 1))
    rows = slice(c0, c0 + FCHUNK)
    b = b_s[rows, lanes]
    mid = b_s[c0 + FCHUNK // 2 - 1:c0 + FCHUNK // 2, lanes]
    last = b_s[c0 + FCHUNK - 1:c0 + FCHUNK, lanes]
    v = z_s[rows, 2 * HGRN_WIDTH + HEAD_DIM * hd:2 * HGRN_WIDTH + HEAD_DIM * (hd + 1)]
    v = v.astype(BF16)
    qs = q_s[rows, lanes] * jnp.exp(b - mid)
    ks = k_s[rows, lanes] * jnp.exp(mid - b)
    a = _dot_nt(qs.astype(BF16), ks.astype(BF16))
    qe = (qs * jnp.exp(mid)).astype(BF16)
    ke = (ks * jnp.exp(last - mid)).astype(BF16)
    return a, qe, _dot_tn(v, ke), jnp.exp(last)


def _hgrn_outputs(c0, hd, st, scores, z_s, g_s, hg, mix_s, tril):
    a, qe, kv, decay = scores
    lanes = slice(HEAD_DIM * hd, HEAD_DIM * (hd + 1))
    rows = slice(c0, c0 + FCHUNK)
    v = z_s[rows, 2 * HGRN_WIDTH + HEAD_DIM * hd:2 * HGRN_WIDTH + HEAD_DIM * (hd + 1)]
    v = v.astype(BF16)
    a = jnp.where(tril, a, 0.0).astype(BF16)
    o = _dot(a, v) + _dot_nt(qe, st.astype(BF16))
    st = st * decay + kv
    ms = jnp.mean(o * o, axis=-1, keepdims=True)
    on = o * lax.rsqrt(ms + EPS) * hg[:, lanes] * g_s[rows, lanes]
    mix_s[rows, lanes] = on.astype(BF16)
    return st


def _hgrn_any_range(z_s, q_s, k_s, g_s, b_s, hg, st_s, mix_s, cb_s, ck_s, cv_s):
    ones_bd = jnp.where(
        lax.broadcasted_iota(jnp.int32, (2 * HEAD_DIM, 2 * HEAD_DIM), 0) // HEAD_DIM
        == lax.broadcasted_iota(jnp.int32, (2 * HEAD_DIM, 2 * HEAD_DIM), 1) // HEAD_DIM,
        1.0, 0.0).astype(BF16)

    def chunk_body(c, carry):
        for half in range(FCHUNK // CHUNK):
            r0 = pl.multiple_of(c * FCHUNK + CHUNK * half, CHUNK)
            rows = pl.ds(r0, CHUNK)
            b = b_s[rows, :]
            if half:
                b = b - b_s[pl.ds(r0 - 1, 1), :]
            q = q_s[rows, :]
            kk = k_s[rows, :]
            v = z_s[rows, 2 * HGRN_WIDTH:3 * HGRN_WIDTH]
            gate = g_s[rows, :]
            cb_s[...] = jnp.stack([b[:, HEAD_DIM * hd:HEAD_DIM * (hd + 1)] for hd in range(HEADS)])
            ck_s[...] = jnp.stack([kk[:, HEAD_DIM * hd:HEAD_DIM * (hd + 1)] for hd in range(HEADS)])
            cv_s[...] = jnp.stack([v[:, HEAD_DIM * hd:HEAD_DIM * (hd + 1)] for hd in range(HEADS)])
            out = {}
            for hd in range(HEADS):
                lanes = slice(HEAD_DIM * hd, HEAD_DIM * (hd + 1))
                out[hd] = _hgrn_head_offdiag(hd, q[:, lanes], kk[:, lanes], v[:, lanes],
                                             b[:, lanes], st_s, cb_s)
            for pair in ((0, 1), (2, 3)):
                _hgrn_pair_diag(pair,
                                [q[:, HEAD_DIM * hd:HEAD_DIM * (hd + 1)] for hd in pair],
                                [b[:, HEAD_DIM * hd:HEAD_DIM * (hd + 1)] for hd in pair],
                                out, cb_s, ck_s, cv_s, ones_bd)
            for hd in range(HEADS):
                lanes = slice(HEAD_DIM * hd, HEAD_DIM * (hd + 1))
                oh = jnp.concatenate(out[hd], axis=0)
                ms = jnp.mean(oh * oh, axis=-1, keepdims=True)
                on = oh * lax.rsqrt(ms + EPS) * hg[:, lanes] * gate[:, lanes]
                mix_s[rows, lanes] = on.astype(BF16)
        return carry

    lax.fori_loop(0, TQ // FCHUNK, chunk_body, 0)


GATE_GROUPS = (1, 0, 3)
OTHER_GROUPS = (2, 5, 6, 4)


def _mixer_kernel(layer, x_ref, lb_ref, w_in_hbm, w_out_hbm, cw_ref, hg_ref, pre_g_ref,
                  post_g_ref, o_ref, w_in_ref, w_out_ref, w_stage, w_sem, h_s, z_s, q_s, k_s, g_s,
                  b_s, mix_s, cbuf, st_s, st_next_s, cb_s, ck_s, cv_s):
    t = pl.program_id(1)
    subs = [SUB_TILE * i for i in range(TQ // SUB_TILE)]

    @pl.when(_first_grid_step())
    def _():
        _load_weight_bf16(w_in_hbm, layer, w_in_ref, w_stage, w_sem, *w_stage.shape[1:])
        _load_weight_bf16(w_out_hbm, layer, w_out_ref, w_stage, w_sem, *w_stage.shape[1:])

    @pl.when(t == 0)
    def _():
        st_s[...] = jnp.zeros_like(st_s)
        cbuf[:, pl.ds(TQ, HALO), :] = jnp.zeros((CONV_WIDTH // LANES, HALO, LANES), F32)

    cbuf[:, 0:HALO, :] = cbuf[:, TQ:TQ + HALO, :]

    def in_project(r0):
        rows = slice(r0, r0 + SUB_TILE)
        h_s[rows, :] = _rmsnorm(x_ref[rows, :], pre_g_ref[...]).astype(BF16)
        for g in GATE_GROUPS + OTHER_GROUPS:
            cols = slice(512 * g, 512 * (g + 1))
            z_s[rows, cols] = _dot(h_s[rows, :], w_in_ref[:, cols])

    consts = _decay_consts(layer, lb_ref)
    tril = (lax.broadcasted_iota(jnp.int32, (FCHUNK, FCHUNK), 0)
            >= lax.broadcasted_iota(jnp.int32, (FCHUNK, FCHUNK), 1))
    tri = jnp.where(tril, 1.0, 0.0).astype(BF16)
    hg = hg_ref[...]
    cw = cw_ref[...]

    def out_project(r0):
        rows = slice(r0, r0 + SUB_TILE)
        mix = _dot(mix_s[rows, :], w_out_ref[...])
        o_ref[rows, :] = x_ref[rows, :] + _rmsnorm(mix, post_g_ref[...])

    state = [st_s[hd] for hd in range(HEADS)]
    span = None
    in_project(subs[0])
    for i, r0 in enumerate(subs):
        rows = slice(r0, r0 + SUB_TILE)
        s = _gate_prep(consts, r0, z_s, q_s, k_s, g_s, b_s, tri)
        span = s if span is None else jnp.maximum(span, s)
        chunk_rows = [r0 + FCHUNK * c for c in range(SUB_TILE // FCHUNK)]
        scores = {(hd, c0): _hgrn_scores(c0, hd, z_s, q_s, k_s, b_s)
                  for hd in range(HEADS) for c0 in chunk_rows}
        for hd in range(HEADS):
            for c0 in chunk_rows:
                state[hd] = _hgrn_outputs(c0, hd, state[hd], scores[hd, c0], z_s, g_s, hg,
                                          mix_s, tril)
        lo = HALO + r0
        for j in range(CONV_WIDTH // LANES):
            lanes = slice(LANES * j, LANES * (j + 1))
            cbuf[j, lo:lo + SUB_TILE, :] = (z_s[rows, 2560 + LANES * j:2560 + LANES * (j + 1)]
                                            * z_s[rows, 3072 + LANES * j:3072 + LANES * (j + 1)])
            y = (cw[2:3, lanes] * cbuf[j, lo:lo + SUB_TILE, :]
                 + cw[1:2, lanes] * cbuf[j, lo - 1:lo - 1 + SUB_TILE, :]
                 + cw[0:1, lanes] * cbuf[j, lo - 2:lo - 2 + SUB_TILE, :])
            gate_b = z_s[rows, 2048 + LANES * j:2048 + LANES * (j + 1)]
            mix_s[rows, HGRN_WIDTH + LANES * j:HGRN_WIDTH + LANES * (j + 1)] = (
                gate_b * y).astype(BF16)
        if i + 1 < len(subs):
            in_project(subs[i + 1])
        if i >= 1:
            out_project(subs[i - 1])
    out_project(subs[-1])
    for hd in range(HEADS):
        st_next_s[hd] = state[hd]

    @pl.when(jnp.logical_not(jnp.max(span) <= MAX_LOG_SPAN))
    def _():
        st_next_s[...] = st_s[...]
        _hgrn_any_range(z_s, q_s, k_s, g_s, b_s, hg, st_next_s, mix_s, cb_s, ck_s, cv_s)
        for r0 in subs:
            out_project(r0)

    st_s[...] = st_next_s[...]


def _ffn_kernel(layer, x_ref, w_up_hbm, cw_ref, w_down_hbm, pre_g_ref, post_g_ref, o_ref,
                w_up_ref, w_down_ref, up_stage, down_stage, w_sem, ubuf):
    t = pl.program_id(1)

    @pl.when(_first_grid_step())
    def _():
        _load_weight_bf16(w_up_hbm, layer, w_up_ref, up_stage, w_sem, *up_stage.shape[1:])
        _load_weight_bf16(w_down_hbm, layer, w_down_ref, down_stage, w_sem, *down_stage.shape[1:])

    @pl.when(t == 0)
    def _():
        ubuf[:, pl.ds(TQ_FFN, HALO), :] = jnp.zeros((2 * D_FF // LANES, HALO, LANES), F32)

    ubuf[:, 0:HALO, :] = ubuf[:, TQ_FFN:TQ_FFN + HALO, :]
    per_block = FF_BLOCK // LANES
    n_blocks = D_FF // FF_BLOCK

    for sub in range(TQ_FFN // SUB_TILE):
        r0 = SUB_TILE * sub
        x = x_ref[r0:r0 + SUB_TILE, :]
        h = _rmsnorm(x, pre_g_ref[...]).astype(BF16)

        def conv(slab):
            w = cw_ref[:, LANES * slab:LANES * (slab + 1)]
            lo = HALO + r0
            return (w[2:3] * ubuf[slab, lo:lo + SUB_TILE, :]
                    + w[1:2] * ubuf[slab, lo - 1:lo - 1 + SUB_TILE, :]
                    + w[0:1] * ubuf[slab, lo - 2:lo - 2 + SUB_TILE, :])

        def up_project(g):
            for base in (FF_BLOCK * g, D_FF + FF_BLOCK * g):
                u = _dot(h, w_up_ref[:, base:base + FF_BLOCK])
                for j in range(per_block):
                    ubuf[base // LANES + j, HALO + r0:HALO + r0 + SUB_TILE, :] = (
                        u[:, LANES * j:LANES * (j + 1)])

        def activate(g):
            act = []
            for j in range(per_block):
                gate = conv(FF_BLOCK * g // LANES + j)
                val = conv((D_FF + FF_BLOCK * g) // LANES + j)
                act.append((gate * _sigmoid(gate) * val).astype(BF16))
            return jnp.concatenate(act, axis=1)

        def down_project(y, g, act):
            part = _dot(act, w_down_ref[FF_BLOCK * g:FF_BLOCK * (g + 1), :])
            return part if y is None else y + part

        y = None
        prev_act = None
        up_project(0)
        for g in range(n_blocks):
            if g + 1 < n_blocks:
                up_project(g + 1)
            if g >= 1:
                y = down_project(y, g - 1, prev_act)
            prev_act = activate(g)
        y = down_project(y, n_blocks - 1, prev_act)

        o_ref[r0:r0 + SUB_TILE, :] = x + _rmsnorm(y, post_g_ref[...])


def _x_spec(tq):
    return pl.BlockSpec((None, tq, D_MODEL), lambda b, t: (b, t, 0))


def _layer_spec(shape, layer):
    return pl.BlockSpec((None,) + shape, lambda b, t: (layer,) + (0,) * len(shape),
                        pipeline_mode=pl.Buffered(1))


_HBM_SPEC = pl.BlockSpec(memory_space=pl.ANY)


_COMPILER_PARAMS = pltpu.CompilerParams(
    dimension_semantics=("arbitrary", "arbitrary"), vmem_limit_bytes=VMEM_LIMIT_BYTES)


def _mixer_call(layer, x, lb_param, w_in, w_out, conv_w, hgrn_g, pre_g, post_g):
    batch, seq, _ = x.shape
    return pl.pallas_call(
        functools.partial(_mixer_kernel, layer),
        out_shape=jax.ShapeDtypeStruct(x.shape, x.dtype),
        grid=(batch, seq // TQ),
        in_specs=[
            _x_spec(TQ),
            pl.BlockSpec((DEPTH, HGRN_WIDTH), lambda b, t: (0, 0)),
            _HBM_SPEC,
            _HBM_SPEC,
            _layer_spec((3, CONV_WIDTH), layer),
            _layer_spec((1, HGRN_WIDTH), layer),
            _layer_spec((1, D_MODEL), layer),
            _layer_spec((1, D_MODEL), layer),
        ],
        out_specs=_x_spec(TQ),
        scratch_shapes=[
            pltpu.VMEM((D_MODEL, IN_COLS), BF16),
            pltpu.VMEM((D_MODEL, D_MODEL), BF16),
            pltpu.VMEM((W_SLOTS,) + W_STAGE, F32),
            pltpu.SemaphoreType.DMA((W_SLOTS,)),
            pltpu.VMEM((TQ, D_MODEL), BF16),
            pltpu.VMEM((TQ, IN_COLS), F32),
            pltpu.VMEM((TQ, HGRN_WIDTH), F32),
            pltpu.VMEM((TQ, HGRN_WIDTH), F32),
            pltpu.VMEM((TQ, HGRN_WIDTH), F32),
            pltpu.VMEM((TQ, HGRN_WIDTH), F32),
            pltpu.VMEM((TQ, D_MODEL), BF16),
            pltpu.VMEM((CONV_WIDTH // LANES, TQ + HALO, LANES), F32),
            pltpu.VMEM((HEADS, HEAD_DIM, HEAD_DIM), F32),
            pltpu.VMEM((HEADS, HEAD_DIM, HEAD_DIM), F32),
            pltpu.VMEM((HEADS, CHUNK, HEAD_DIM), F32),
            pltpu.VMEM((HEADS, CHUNK, HEAD_DIM), F32),
            pltpu.VMEM((HEADS, CHUNK, HEAD_DIM), F32),
        ],
        compiler_params=_COMPILER_PARAMS,
        name=f"mixer_l{layer}",
    )(x, lb_param, w_in, w_out, conv_w, hgrn_g, pre_g, post_g)


def _ffn_call(layer, x, w_up, conv_w, w_down, pre_g, post_g):
    batch, seq, _ = x.shape
    return pl.pallas_call(
        functools.partial(_ffn_kernel, layer),
        out_shape=jax.ShapeDtypeStruct(x.shape, x.dtype),
        grid=(batch, seq // TQ_FFN),
        in_specs=[
            _x_spec(TQ_FFN),
            _HBM_SPEC,
            _layer_spec((3, 2 * D_FF), layer),
            _HBM_SPEC,
            _layer_spec((1, D_MODEL), layer),
            _layer_spec((1, D_MODEL), layer),
        ],
        out_specs=_x_spec(TQ_FFN),
        scratch_shapes=[
            pltpu.VMEM((D_MODEL, 2 * D_FF), BF16),
            pltpu.VMEM((D_FF, D_MODEL), BF16),
            pltpu.VMEM((W_SLOTS,) + W_STAGE, F32),
            pltpu.VMEM((W_SLOTS,) + W_STAGE_DOWN, F32),
            pltpu.SemaphoreType.DMA((W_SLOTS,)),
            pltpu.VMEM((2 * D_FF // LANES, TQ_FFN + HALO, LANES), F32),
        ],
        compiler_params=_COMPILER_PARAMS,
        name=f"ffn_l{layer}",
    )(x, w_up, conv_w, w_down, pre_g, post_g)


def kernel(x, lb_param, w_in, w_out, conv_w, ffn_w_up, ffn_conv_w, ffn_w_down, hgrn_norm_g,
           pre_mix_g, post_mix_g, pre_ffn_g, post_ffn_g):
    assert x.shape[1] % TQ == 0 and x.shape[1] % TQ_FFN == 0 and x.shape[2] == D_MODEL
    hgrn_g = hgrn_norm_g[:, None, :]
    pre_mix, post_mix = pre_mix_g[:, None, :], post_mix_g[:, None, :]
    pre_ffn, post_ffn = pre_ffn_g[:, None, :], post_ffn_g[:, None, :]
    for layer in range(DEPTH):
        x = _mixer_call(layer, x, lb_param, w_in, w_out, conv_w, hgrn_g, pre_mix, post_mix)
        x = _ffn_call(layer, x, ffn_w_up, ffn_conv_w, ffn_w_down, pre_ffn, post_ffn)
    return x
```

```python
import functools

import jax
import jax.numpy as jnp
from jax import lax
from jax.experimental import pallas as pl
from jax.experimental.pallas import tpu as pltpu

D_MODEL = 1024
DEPTH = 4
HGRN_WIDTH = 512
HEADS = 4
HEAD_DIM = 128
CONV_WIDTH = 512
IN_COLS = 4 * HGRN_WIDTH + 3 * CONV_WIDTH
D_FF = 2816
EPS = 1e-6

TQ = 512
TQ_FFN = 512
SUB_TILE = 256
FCHUNK = 128
MAX_LOG_SPAN = 80.0
SCAN_PAD = FCHUNK // 2
CHUNK = 64
SUB = 16
NSUB = CHUNK // SUB
HALO = 8
LANES = 128
FF_BLOCK = 256
NEG_BIG = -1e30
W_SLOTS = 3
W_STAGE = (1024, 512)
W_STAGE_DOWN = (256, 1024)
VMEM_LIMIT_BYTES = 56 * 1024 * 1024

F32 = jnp.float32
BF16 = jnp.bfloat16


def _dot(a, b):
    return jnp.dot(a, b, preferred_element_type=F32)


def _dot_nt(a, b):
    return lax.dot_general(a, b, (((1,), (1,)), ((), ())), preferred_element_type=F32)


def _dot_tn(a, b):
    return lax.dot_general(a, b, (((0,), (0,)), ((), ())), preferred_element_type=F32)


def _rmsnorm(x, g):
    ms = jnp.mean(x * x, axis=-1, keepdims=True)
    return x * lax.rsqrt(ms + EPS) * g


def _sigmoid(x):
    return 1.0 / (1.0 + jnp.exp(-x))


def _first_grid_step():
    return jnp.logical_and(pl.program_id(0) == 0, pl.program_id(1) == 0)


def _load_weight_bf16(w_hbm, layer, w_bf, stage, sem, row_chunk, col_chunk):
    rows, cols = w_bf.shape
    n_slots = stage.shape[0]
    pieces = [(row_chunk * r, col_chunk * c)
              for r in range(rows // row_chunk) for c in range(cols // col_chunk)]

    def copy(i):
        r0, c0 = pieces[i]
        src = w_hbm.at[layer, pl.ds(r0, row_chunk), pl.ds(c0, col_chunk)]
        return pltpu.make_async_copy(src, stage.at[i % n_slots], sem.at[i % n_slots])

    for i in range(min(n_slots, len(pieces))):
        copy(i).start()
    for i, (r0, c0) in enumerate(pieces):
        copy(i).wait()
        w_bf[r0:r0 + row_chunk, c0:c0 + col_chunk] = stage[i % n_slots].astype(BF16)
        if i + n_slots < len(pieces):
            copy(i + n_slots).start()


def _layer_lower_bound(lb_ref, layer):
    lbp = lb_ref[...]
    m = jnp.max(lbp, axis=0, keepdims=True)
    e = jnp.exp(lbp - m)
    p = e / jnp.sum(e, axis=0, keepdims=True)
    lb = p[1:2]
    for j in range(2, layer + 1):
        lb = lb + p[j:j + 1]
    return lb


def _bcast_row(ref, head, row, nrows):
    return jnp.broadcast_to(ref[head, pl.ds(row, 1), :], (nrows, HEAD_DIM))


def _hgrn_head_offdiag(head, q, k, v, b, st_ref, b_s):
    blk = [slice(SUB * i, SUB * (i + 1)) for i in range(NSUB)]
    bend = [_bcast_row(b_s, head, SUB * i + SUB - 1, SUB) for i in range(NSUB)]
    qp = [q[blk[0]] * jnp.exp(b[blk[0]])]
    for i in range(1, NSUB):
        qp.append(q[blk[i]] * jnp.exp(b[blk[i]] - bend[i - 1]))
    kp = [k[blk[i]] * jnp.exp(bend[i] - b[blk[i]]) for i in range(NSUB)]

    st = st_ref[head]
    st_b = st.astype(BF16)
    qb = [qp[0]] + [qp[i] * jnp.exp(bend[i - 1]) for i in range(1, NSUB)]
    o_inter = _dot_nt(jnp.concatenate(qb, axis=0).astype(BF16), st_b)
    out = [o_inter[blk[i]] for i in range(NSUB)]

    v_b = v.astype(BF16)
    for i in range(1, NSUB):
        parts = [kp[j] * jnp.exp(bend[i - 1] - bend[j]) for j in range(i - 1)] + [kp[i - 1]]
        kcat = jnp.concatenate(parts, axis=0) if len(parts) > 1 else parts[0]
        a = _dot_nt(qp[i].astype(BF16), kcat.astype(BF16))
        out[i] = out[i] + _dot(a.astype(BF16), v_b[0:SUB * i])

    last = NSUB - 1
    kend = jnp.concatenate(
        [kp[j] * jnp.exp(bend[last] - bend[j]) for j in range(last)] + [kp[last]], axis=0)
    decay = jnp.exp(_bcast_row(b_s, head, CHUNK - 1, HEAD_DIM))
    st_ref[head] = st * decay + _dot_tn(v_b, kend.astype(BF16))
    return out


def _hgrn_pair_diag(heads, qs, bs, out, b_s, k_s, v_s, ones_bd):
    row = lax.broadcasted_iota(jnp.int32, (8, HEAD_DIM), 0)
    for i in range(NSUB):
        pieces = {h: [] for h in heads}
        for h, q, b in zip(heads, qs, bs):
            q_lo, q_hi = q[SUB * i:SUB * i + 8], q[SUB * i + 8:SUB * (i + 1)]
            b_lo, b_hi = b[SUB * i:SUB * i + 8], b[SUB * i + 8:SUB * (i + 1)]
            for s in range(SUB):
                r = SUB * i + s
                b_row = _bcast_row(b_s, h, r, 8)
                k_row = _bcast_row(k_s, h, r, 8)
                if s < 8:
                    d_lo = jnp.where(row >= s, b_lo - b_row, NEG_BIG)
                    pieces[h].append(q_lo * jnp.exp(d_lo) * k_row)
                    pieces[h].append(q_hi * jnp.exp(b_hi - b_row) * k_row)
                else:
                    d_hi = jnp.where(row >= s - 8, b_hi - b_row, NEG_BIG)
                    pieces[h].append(q_hi * jnp.exp(d_hi) * k_row)
        p = jnp.concatenate(
            [jnp.concatenate(pieces[h], axis=0) for h in heads], axis=1).astype(BF16)
        red = _dot(p, ones_bd)
        for hi, h in enumerate(heads):
            lanes = slice(HEAD_DIM * hi, HEAD_DIM * (hi + 1))
            acc_lo = out[h][i][0:8]
            acc_hi = out[h][i][8:SUB]
            n = 0
            for s in range(SUB):
                v_row = _bcast_row(v_s, h, SUB * i + s, 8)
                if s < 8:
                    acc_lo = acc_lo + red[8 * n:8 * n + 8, lanes] * v_row
                    n += 1
                acc_hi = acc_hi + red[8 * n:8 * n + 8, lanes] * v_row
                n += 1
            out[h][i] = jnp.concatenate([acc_lo, acc_hi], axis=0)


def _decay_consts(layer, lb_ref):
    if layer == 0:
        return None
    lb = _layer_lower_bound(lb_ref, layer)
    return jnp.log(lb), jnp.log1p(-lb), 1.0 - lb


def _chunk_cumsum(log_f, scan_s, slot, hd, b_s, c0):
    lanes = slice(HEAD_DIM * hd, HEAD_DIM * (hd + 1))
    body = slice(SCAN_PAD, SCAN_PAD + FCHUNK)
    scan_s[0, slot, hd, body, :] = log_f
    src = 0
    shift = 1
    while shift < FCHUNK:
        total = (scan_s[src, slot, hd, body, :]
                 + scan_s[src, slot, hd, SCAN_PAD - shift:SCAN_PAD - shift + FCHUNK, :])
        if 2 * shift < FCHUNK:
            scan_s[1 - src, slot, hd, body, :] = total
        else:
            b_s[c0:c0 + FCHUNK, lanes] = total
        src = 1 - src
        shift *= 2


def _gate_prep(consts, r0, z_s, q_s, k_s, g_s, b_s, scan_s):
    rows = slice(r0, r0 + SUB_TILE)
    q_abs = []
    for hd in range(HEADS):
        lanes = slice(HEAD_DIM * hd, HEAD_DIM * (hd + 1))
        fp = z_s[rows, HGRN_WIDTH + HEAD_DIM * hd:HGRN_WIDTH + HEAD_DIM * (hd + 1)]
        e = jnp.exp(-jnp.abs(fp))
        log_sig = jnp.minimum(fp, 0.0) - jnp.log(1.0 + e)
        r = 1.0 / (1.0 + e)
        sig_neg = jnp.where(fp >= 0.0, e * r, r)
        if consts is None:
            log_f, kk = log_sig, sig_neg
        else:
            log_lb, log_1m, one_m = consts
            cc = log_1m[:, lanes] + log_sig
            la = log_lb[:, lanes]
            log_f = jnp.maximum(la, cc) + jnp.log(1.0 + jnp.exp(-jnp.abs(la - cc)))
            kk = one_m[:, lanes] * sig_neg
        k_s[rows, lanes] = kk
        for c in range(SUB_TILE // FCHUNK):
            c0 = r0 + FCHUNK * c
            _chunk_cumsum(log_f[FCHUNK * c:FCHUNK * (c + 1)], scan_s, c0 // FCHUNK, hd, b_s, c0)
        qz = z_s[rows, lanes]
        q = qz * _sigmoid(qz) * (HEAD_DIM ** -0.5)
        q_s[rows, lanes] = q
        q_abs.append(jnp.max(jnp.abs(q), axis=0, keepdims=True))
        gz = z_s[rows, 3 * HGRN_WIDTH + HEAD_DIM * hd:3 * HGRN_WIDTH + HEAD_DIM * (hd + 1)]
        g_s[rows, lanes] = gz * _sigmoid(gz)

    span = None
    for c in range(SUB_TILE // FCHUNK):
        c0 = r0 + FCHUNK * c
        mid = b_s[c0 + FCHUNK // 2 - 1:c0 + FCHUNK // 2, :]
        s = jnp.maximum(b_s[c0:c0 + 1, :] - mid, mid - b_s[c0 + FCHUNK - 1:c0 + FCHUNK, :])
        span = s if span is None else jnp.maximum(span, s)
    return span + jnp.maximum(jnp.log(jnp.concatenate(q_abs, axis=1)), 0.0)


def _hgrn_scores(c0, hd, z_s, q_s, k_s, b_s):
    lanes = slice(HEAD_DIM * hd, HEAD_DIM * (hd + 1))
    rows = slice(c0, c0 + FCHUNK)
    b = b_s[rows, lanes]
    mid = b_s[c0 + FCHUNK // 2 - 1:c0 + FCHUNK // 2, lanes]
    last = b_s[c0 + FCHUNK - 1:c0 + FCHUNK, lanes]
    v = z_s[rows, 2 * HGRN_WIDTH + HEAD_DIM * hd:2 * HGRN_WIDTH + HEAD_DIM * (hd + 1)]
    v = v.astype(BF16)
    qs = q_s[rows, lanes] * jnp.exp(b - mid)
    ks = k_s[rows, lanes] * jnp.exp(mid - b)
    a = _dot_nt(qs.astype(BF16), ks.astype(BF16))
    qe = (qs * jnp.exp(mid)).astype(BF16)
    ke = (ks * jnp.exp(last - mid)).astype(BF16)
    return a, qe, _dot_tn(v, ke), jnp.exp(last)


def _hgrn_outputs(c0, hd, st, scores, z_s, g_s, hg, mix_s, tril):
    a, qe, kv, decay = scores
    lanes = slice(HEAD_DIM * hd, HEAD_DIM * (hd + 1))
    rows = slice(c0, c0 + FCHUNK)
    v = z_s[rows, 2 * HGRN_WIDTH + HEAD_DIM * hd:2 * HGRN_WIDTH + HEAD_DIM * (hd + 1)]
    v = v.astype(BF16)
    a = jnp.where(tril, a, 0.0).astype(BF16)
    o = _dot(a, v) + _dot_nt(qe, st.astype(BF16))
    st = st * decay + kv
    ms = jnp.mean(o * o, axis=-1, keepdims=True)
    on = o * lax.rsqrt(ms + EPS) * hg[:, lanes] * g_s[rows, lanes]
    mix_s[rows, lanes] = on.astype(BF16)
    return st


def _hgrn_any_range(z_s, q_s, k_s, g_s, b_s, hg, st_s, mix_s, cb_s, ck_s, cv_s):
    ones_bd = jnp.where(
        lax.broadcasted_iota(jnp.int32, (2 * HEAD_DIM, 2 * HEAD_DIM), 0) // HEAD_DIM
        == lax.broadcasted_iota(jnp.int32, (2 * HEAD_DIM, 2 * HEAD_DIM), 1) // HEAD_DIM,
        1.0, 0.0).astype(BF16)

    def chunk_body(c, carry):
        for half in range(FCHUNK // CHUNK):
            r0 = pl.multiple_of(c * FCHUNK + CHUNK * half, CHUNK)
            rows = pl.ds(r0, CHUNK)
            b = b_s[rows, :]
            if half:
                b = b - b_s[pl.ds(r0 - 1, 1), :]
            q = q_s[rows, :]
            kk = k_s[rows, :]
            v = z_s[rows, 2 * HGRN_WIDTH:3 * HGRN_WIDTH]
            gate = g_s[rows, :]
            cb_s[...] = jnp.stack([b[:, HEAD_DIM * hd:HEAD_DIM * (hd + 1)] for hd in range(HEADS)])
            ck_s[...] = jnp.stack([kk[:, HEAD_DIM * hd:HEAD_DIM * (hd + 1)] for hd in range(HEADS)])
            cv_s[...] = jnp.stack([v[:, HEAD_DIM * hd:HEAD_DIM * (hd + 1)] for hd in range(HEADS)])
            out = {}
            for hd in range(HEADS):
                lanes = slice(HEAD_DIM * hd, HEAD_DIM * (hd + 1))
                out[hd] = _hgrn_head_offdiag(hd, q[:, lanes], kk[:, lanes], v[:, lanes],
                                             b[:, lanes], st_s, cb_s)
            for pair in ((0, 1), (2, 3)):
                _hgrn_pair_diag(pair,
                                [q[:, HEAD_DIM * hd:HEAD_DIM * (hd + 1)] for hd in pair],
                                [b[:, HEAD_DIM * hd:HEAD_DIM * (hd + 1)] for hd in pair],
                                out, cb_s, ck_s, cv_s, ones_bd)
            for hd in range(HEADS):
                lanes = slice(HEAD_DIM * hd, HEAD_DIM * (hd + 1))
                oh = jnp.concatenate(out[hd], axis=0)
                ms = jnp.mean(oh * oh, axis=-1, keepdims=True)
                on = oh * lax.rsqrt(ms + EPS) * hg[:, lanes] * gate[:, lanes]
                mix_s[rows, lanes] = on.astype(BF16)
        return carry

    lax.fori_loop(0, TQ // FCHUNK, chunk_body, 0)


GATE_GROUPS = (1, 0, 3)
OTHER_GROUPS = (2, 5, 6, 4)


def _mixer_kernel(layer, x_ref, lb_ref, w_in_hbm, w_out_hbm, cw_ref, hg_ref, pre_g_ref,
                  post_g_ref, o_ref, w_in_ref, w_out_ref, w_stage, w_sem, h_s, z_s, q_s, k_s, g_s,
                  b_s, scan_s, mix_s, cbuf, st_s, st_next_s, cb_s, ck_s, cv_s):
    t = pl.program_id(1)
    subs = [SUB_TILE * i for i in range(TQ // SUB_TILE)]

    @pl.when(_first_grid_step())
    def _():
        _load_weight_bf16(w_in_hbm, layer, w_in_ref, w_stage, w_sem, *w_stage.shape[1:])
        _load_weight_bf16(w_out_hbm, layer, w_out_ref, w_stage, w_sem, *w_stage.shape[1:])
        scan_s[:, :, :, 0:SCAN_PAD, :] = jnp.zeros(scan_s.shape[:3] + (SCAN_PAD, HEAD_DIM), F32)

    @pl.when(t == 0)
    def _():
        st_s[...] = jnp.zeros_like(st_s)
        cbuf[:, pl.ds(TQ, HALO), :] = jnp.zeros((CONV_WIDTH // LANES, HALO, LANES), F32)

    cbuf[:, 0:HALO, :] = cbuf[:, TQ:TQ + HALO, :]

    def in_project(r0):
        rows = slice(r0, r0 + SUB_TILE)
        h_s[rows, :] = _rmsnorm(x_ref[rows, :], pre_g_ref[...]).astype(BF16)
        for g in GATE_GROUPS + OTHER_GROUPS:
            cols = slice(512 * g, 512 * (g + 1))
            z_s[rows, cols] = _dot(h_s[rows, :], w_in_ref[:, cols])

    consts = _decay_consts(layer, lb_ref)
    tril = (lax.broadcasted_iota(jnp.int32, (FCHUNK, FCHUNK), 0)
            >= lax.broadcasted_iota(jnp.int32, (FCHUNK, FCHUNK), 1))
    hg = hg_ref[...]
    cw = cw_ref[...]

    def out_project(r0):
        rows = slice(r0, r0 + SUB_TILE)
        mix = _dot(mix_s[rows, :], w_out_ref[...])
        o_ref[rows, :] = x_ref[rows, :] + _rmsnorm(mix, post_g_ref[...])

    state = [st_s[hd] for hd in range(HEADS)]
    span = None
    in_project(subs[0])
    for i, r0 in enumerate(subs):
        rows = slice(r0, r0 + SUB_TILE)
        s = _gate_prep(consts, r0, z_s, q_s, k_s, g_s, b_s, scan_s)
        span = s if span is None else jnp.maximum(span, s)
        chunk_rows = [r0 + FCHUNK * c for c in range(SUB_TILE // FCHUNK)]
        scores = {(hd, c0): _hgrn_scores(c0, hd, z_s, q_s, k_s, b_s)
                  for hd in range(HEADS) for c0 in chunk_rows}
        for hd in range(HEADS):
            for c0 in chunk_rows:
                state[hd] = _hgrn_outputs(c0, hd, state[hd], scores[hd, c0], z_s, g_s, hg,
                                          mix_s, tril)
        lo = HALO + r0
        for j in range(CONV_WIDTH // LANES):
            lanes = slice(LANES * j, LANES * (j + 1))
            cbuf[j, lo:lo + SUB_TILE, :] = (z_s[rows, 2560 + LANES * j:2560 + LANES * (j + 1)]
                                            * z_s[rows, 3072 + LANES * j:3072 + LANES * (j + 1)])
            y = (cw[2:3, lanes] * cbuf[j, lo:lo + SUB_TILE, :]
                 + cw[1:2, lanes] * cbuf[j, lo - 1:lo - 1 + SUB_TILE, :]
                 + cw[0:1, lanes] * cbuf[j, lo - 2:lo - 2 + SUB_TILE, :])
            gate_b = z_s[rows, 2048 + LANES * j:2048 + LANES * (j + 1)]
            mix_s[rows, HGRN_WIDTH + LANES * j:HGRN_WIDTH + LANES * (j + 1)] = (
                gate_b * y).astype(BF16)
        if i + 1 < len(subs):
            in_project(subs[i + 1])
        if i >= 1:
            out_project(subs[i - 1])
    out_project(subs[-1])
    for hd in range(HEADS):
        st_next_s[hd] = state[hd]

    @pl.when(jnp.logical_not(jnp.max(span) <= MAX_LOG_SPAN))
    def _():
        st_next_s[...] = st_s[...]
        _hgrn_any_range(z_s, q_s, k_s, g_s, b_s, hg, st_next_s, mix_s, cb_s, ck_s, cv_s)
        for r0 in subs:
            out_project(r0)

    st_s[...] = st_next_s[...]


def _ffn_kernel(layer, x_ref, w_up_hbm, cw_ref, w_down_hbm, pre_g_ref, post_g_ref, o_ref,
                w_up_ref, w_down_ref, up_stage, down_stage, w_sem, ubuf):
    t = pl.program_id(1)

    @pl.when(_first_grid_step())
    def _():
        _load_weight_bf16(w_up_hbm, layer, w_up_ref, up_stage, w_sem, *up_stage.shape[1:])
        _load_weight_bf16(w_down_hbm, layer, w_down_ref, down_stage, w_sem, *down_stage.shape[1:])

    @pl.when(t == 0)
    def _():
        ubuf[:, pl.ds(TQ_FFN, HALO), :] = jnp.zeros((2 * D_FF // LANES, HALO, LANES), F32)

    ubuf[:, 0:HALO, :] = ubuf[:, TQ_FFN:TQ_FFN + HALO, :]
    per_block = FF_BLOCK // LANES
    n_blocks = D_FF // FF_BLOCK

    for sub in range(TQ_FFN // SUB_TILE):
        r0 = SUB_TILE * sub
        x = x_ref[r0:r0 + SUB_TILE, :]
        h = _rmsnorm(x, pre_g_ref[...]).astype(BF16)

        def conv(slab):
            w = cw_ref[:, LANES * slab:LANES * (slab + 1)]
            lo = HALO + r0
            return (w[2:3] * ubuf[slab, lo:lo + SUB_TILE, :]
                    + w[1:2] * ubuf[slab, lo - 1:lo - 1 + SUB_TILE, :]
                    + w[0:1] * ubuf[slab, lo - 2:lo - 2 + SUB_TILE, :])

        def up_project(g):
            for base in (FF_BLOCK * g, D_FF + FF_BLOCK * g):
                u = _dot(h, w_up_ref[:, base:base + FF_BLOCK])
                for j in range(per_block):
                    ubuf[base // LANES + j, HALO + r0:HALO + r0 + SUB_TILE, :] = (
                        u[:, LANES * j:LANES * (j + 1)])

        def activate(g):
            act = []
            for j in range(per_block):
                gate = conv(FF_BLOCK * g // LANES + j)
                val = conv((D_FF + FF_BLOCK * g) // LANES + j)
                act.append((gate * _sigmoid(gate) * val).astype(BF16))
            return jnp.concatenate(act, axis=1)

        def down_project(y, g, act):
            part = _dot(act, w_down_ref[FF_BLOCK * g:FF_BLOCK * (g + 1), :])
            return part if y is None else y + part

        y = None
        prev_act = None
        up_project(0)
        for g in range(n_blocks):
            if g + 1 < n_blocks:
                up_project(g + 1)
            if g >= 1:
                y = down_project(y, g - 1, prev_act)
            prev_act = activate(g)
        y = down_project(y, n_blocks - 1, prev_act)

        o_ref[r0:r0 + SUB_TILE, :] = x + _rmsnorm(y, post_g_ref[...])


def _x_spec(tq):
    return pl.BlockSpec((None, tq, D_MODEL), lambda b, t: (b, t, 0))


def _layer_spec(shape, layer):
    return pl.BlockSpec((None,) + shape, lambda b, t: (layer,) + (0,) * len(shape),
                        pipeline_mode=pl.Buffered(1))


_HBM_SPEC = pl.BlockSpec(memory_space=pl.ANY)


_COMPILER_PARAMS = pltpu.CompilerParams(
    dimension_semantics=("arbitrary", "arbitrary"), vmem_limit_bytes=VMEM_LIMIT_BYTES)


def _mixer_call(layer, x, lb_param, w_in, w_out, conv_w, hgrn_g, pre_g, post_g):
    batch, seq, _ = x.shape
    return pl.pallas_call(
        functools.partial(_mixer_kernel, layer),
        out_shape=jax.ShapeDtypeStruct(x.shape, x.dtype),
        grid=(batch, seq // TQ),
        in_specs=[
            _x_spec(TQ),
            pl.BlockSpec((DEPTH, HGRN_WIDTH), lambda b, t: (0, 0)),
            _HBM_SPEC,
            _HBM_SPEC,
            _layer_spec((3, CONV_WIDTH), layer),
            _layer_spec((1, HGRN_WIDTH), layer),
            _layer_spec((1, D_MODEL), layer),
            _layer_spec((1, D_MODEL), layer),
        ],
        out_specs=_x_spec(TQ),
        scratch_shapes=[
            pltpu.VMEM((D_MODEL, IN_COLS), BF16),
            pltpu.VMEM((D_MODEL, D_MODEL), BF16),
            pltpu.VMEM((W_SLOTS,) + W_STAGE, F32),
            pltpu.SemaphoreType.DMA((W_SLOTS,)),
            pltpu.VMEM((TQ, D_MODEL), BF16),
            pltpu.VMEM((TQ, IN_COLS), F32),
            pltpu.VMEM((TQ, HGRN_WIDTH), F32),
            pltpu.VMEM((TQ, HGRN_WIDTH), F32),
            pltpu.VMEM((TQ, HGRN_WIDTH), F32),
            pltpu.VMEM((TQ, HGRN_WIDTH), F32),
            pltpu.VMEM((2, TQ // FCHUNK, HEADS, SCAN_PAD + FCHUNK, HEAD_DIM), F32),
            pltpu.VMEM((TQ, D_MODEL), BF16),
            pltpu.VMEM((CONV_WIDTH // LANES, TQ + HALO, LANES), F32),
            pltpu.VMEM((HEADS, HEAD_DIM, HEAD_DIM), F32),
            pltpu.VMEM((HEADS, HEAD_DIM, HEAD_DIM), F32),
            pltpu.VMEM((HEADS, CHUNK, HEAD_DIM), F32),
            pltpu.VMEM((HEADS, CHUNK, HEAD_DIM), F32),
            pltpu.VMEM((HEADS, CHUNK, HEAD_DIM), F32),
        ],
        compiler_params=_COMPILER_PARAMS,
        name=f"mixer_l{layer}",
    )(x, lb_param, w_in, w_out, conv_w, hgrn_g, pre_g, post_g)


def _ffn_call(layer, x, w_up, conv_w, w_down, pre_g, post_g):
    batch, seq, _ = x.shape
    return pl.pallas_call(
        functools.partial(_ffn_kernel, layer),
        out_shape=jax.ShapeDtypeStruct(x.shape, x.dtype),
        grid=(batch, seq // TQ_FFN),
        in_specs=[
            _x_spec(TQ_FFN),
            _HBM_SPEC,
            _layer_spec((3, 2 * D_FF), layer),
            _HBM_SPEC,
            _layer_spec((1, D_MODEL), layer),
            _layer_spec((1, D_MODEL), layer),
        ],
        out_specs=_x_spec(TQ_FFN),
        scratch_shapes=[
            pltpu.VMEM((D_MODEL, 2 * D_FF), BF16),
            pltpu.VMEM((D_FF, D_MODEL), BF16),
            pltpu.VMEM((W_SLOTS,) + W_STAGE, F32),
            pltpu.VMEM((W_SLOTS,) + W_STAGE_DOWN, F32),
            pltpu.SemaphoreType.DMA((W_SLOTS,)),
            pltpu.VMEM((2 * D_FF // LANES, TQ_FFN + HALO, LANES), F32),
        ],
        compiler_params=_COMPILER_PARAMS,
        name=f"ffn_l{layer}",
    )(x, w_up, conv_w, w_down, pre_g, post_g)


def kernel(x, lb_param, w_in, w_out, conv_w, ffn_w_up, ffn_conv_w, ffn_w_down, hgrn_norm_g,
           pre_mix_g, post_mix_g, pre_ffn_g, post_ffn_g):
    assert x.shape[1] % TQ == 0 and x.shape[1] % TQ_FFN == 0 and x.shape[2] == D_MODEL
    hgrn_g = hgrn_norm_g[:, None, :]
    pre_mix, post_mix = pre_mix_g[:, None, :], post_mix_g[:, None, :]
    pre_ffn, post_ffn = pre_ffn_g[:, None, :], post_ffn_g[:, None, :]
    for layer in range(DEPTH):
        x = _mixer_call(layer, x, lb_param, w_in, w_out, conv_w, hgrn_g, pre_mix, post_mix)
        x = _ffn_call(layer, x, ffn_w_up, ffn_conv_w, ffn_w_down, pre_ffn, post_ffn)
    return x
```

```python
import functools

import jax
import jax.numpy as jnp
from jax import lax
from jax.experimental import pallas as pl
from jax.experimental.pallas import tpu as pltpu

D_MODEL = 1024
DEPTH = 4
HGRN_WIDTH = 512
HEADS = 4
HEAD_DIM = 128
CONV_WIDTH = 512
IN_COLS = 4 * HGRN_WIDTH + 3 * CONV_WIDTH
D_FF = 2816
EPS = 1e-6

TQ = 512
TQ_FFN = 512
SUB_TILE = 256
FCHUNK = 128
MAX_LOG_SPAN = 80.0
CHUNK = 64
SUB = 16
NSUB = CHUNK // SUB
HALO = 8
LANES = 128
FF_BLOCK = 256
NEG_BIG = -1e30
NEG_LOG2E = -1.4426950408889634
W_SLOTS = 3
W_STAGE = (1024, 512)
W_STAGE_DOWN = (256, 1024)
VMEM_LIMIT_BYTES = 56 * 1024 * 1024

F32 = jnp.float32
BF16 = jnp.bfloat16


def _dot(a, b):
    return jnp.dot(a, b, preferred_element_type=F32)


def _dot_nt(a, b):
    return lax.dot_general(a, b, (((1,), (1,)), ((), ())), preferred_element_type=F32)


def _dot_tn(a, b):
    return lax.dot_general(a, b, (((0,), (0,)), ((), ())), preferred_element_type=F32)


def _rmsnorm(x, g):
    ms = jnp.mean(x * x, axis=-1, keepdims=True)
    return x * lax.rsqrt(ms + EPS) * g


def _sigmoid(x):
    return 1.0 / (1.0 + jnp.exp(-x))


def _silu(x, scale=1.0):
    half = x * (0.5 * scale)
    return half + half * jnp.tanh(0.5 * x)


def _first_grid_step():
    return jnp.logical_and(pl.program_id(0) == 0, pl.program_id(1) == 0)


def _load_weight_bf16(w_hbm, layer, w_bf, stage, sem, row_chunk, col_chunk):
    rows, cols = w_bf.shape
    n_slots = stage.shape[0]
    pieces = [(row_chunk * r, col_chunk * c)
              for r in range(rows // row_chunk) for c in range(cols // col_chunk)]

    def copy(i):
        r0, c0 = pieces[i]
        src = w_hbm.at[layer, pl.ds(r0, row_chunk), pl.ds(c0, col_chunk)]
        return pltpu.make_async_copy(src, stage.at[i % n_slots], sem.at[i % n_slots])

    for i in range(min(n_slots, len(pieces))):
        copy(i).start()
    for i, (r0, c0) in enumerate(pieces):
        copy(i).wait()
        w_bf[r0:r0 + row_chunk, c0:c0 + col_chunk] = stage[i % n_slots].astype(BF16)
        if i + n_slots < len(pieces):
            copy(i + n_slots).start()


def _layer_lower_bound(lb_ref, layer):
    lbp = lb_ref[...]
    m = jnp.max(lbp, axis=0, keepdims=True)
    e = jnp.exp(lbp - m)
    p = e / jnp.sum(e, axis=0, keepdims=True)
    lb = p[1:2]
    for j in range(2, layer + 1):
        lb = lb + p[j:j + 1]
    return lb


def _bcast_row(ref, head, row, nrows):
    return jnp.broadcast_to(ref[head, pl.ds(row, 1), :], (nrows, HEAD_DIM))


def _hgrn_head_offdiag(head, q, k, v, b, st_ref, b_s):
    blk = [slice(SUB * i, SUB * (i + 1)) for i in range(NSUB)]
    bend = [_bcast_row(b_s, head, SUB * i + SUB - 1, SUB) for i in range(NSUB)]
    qp = [q[blk[0]] * jnp.exp(b[blk[0]])]
    for i in range(1, NSUB):
        qp.append(q[blk[i]] * jnp.exp(b[blk[i]] - bend[i - 1]))
    kp = [k[blk[i]] * jnp.exp(bend[i] - b[blk[i]]) for i in range(NSUB)]

    st = st_ref[head]
    st_b = st.astype(BF16)
    qb = [qp[0]] + [qp[i] * jnp.exp(bend[i - 1]) for i in range(1, NSUB)]
    o_inter = _dot_nt(jnp.concatenate(qb, axis=0).astype(BF16), st_b)
    out = [o_inter[blk[i]] for i in range(NSUB)]

    v_b = v.astype(BF16)
    for i in range(1, NSUB):
        parts = [kp[j] * jnp.exp(bend[i - 1] - bend[j]) for j in range(i - 1)] + [kp[i - 1]]
        kcat = jnp.concatenate(parts, axis=0) if len(parts) > 1 else parts[0]
        a = _dot_nt(qp[i].astype(BF16), kcat.astype(BF16))
        out[i] = out[i] + _dot(a.astype(BF16), v_b[0:SUB * i])

    last = NSUB - 1
    kend = jnp.concatenate(
        [kp[j] * jnp.exp(bend[last] - bend[j]) for j in range(last)] + [kp[last]], axis=0)
    decay = jnp.exp(_bcast_row(b_s, head, CHUNK - 1, HEAD_DIM))
    st_ref[head] = st * decay + _dot_tn(v_b, kend.astype(BF16))
    return out


def _hgrn_pair_diag(heads, qs, bs, out, b_s, k_s, v_s, ones_bd):
    row = lax.broadcasted_iota(jnp.int32, (8, HEAD_DIM), 0)
    for i in range(NSUB):
        pieces = {h: [] for h in heads}
        for h, q, b in zip(heads, qs, bs):
            q_lo, q_hi = q[SUB * i:SUB * i + 8], q[SUB * i + 8:SUB * (i + 1)]
            b_lo, b_hi = b[SUB * i:SUB * i + 8], b[SUB * i + 8:SUB * (i + 1)]
            for s in range(SUB):
                r = SUB * i + s
                b_row = _bcast_row(b_s, h, r, 8)
                k_row = _bcast_row(k_s, h, r, 8)
                if s < 8:
                    d_lo = jnp.where(row >= s, b_lo - b_row, NEG_BIG)
                    pieces[h].append(q_lo * jnp.exp(d_lo) * k_row)
                    pieces[h].append(q_hi * jnp.exp(b_hi - b_row) * k_row)
                else:
                    d_hi = jnp.where(row >= s - 8, b_hi - b_row, NEG_BIG)
                    pieces[h].append(q_hi * jnp.exp(d_hi) * k_row)
        p = jnp.concatenate(
            [jnp.concatenate(pieces[h], axis=0) for h in heads], axis=1).astype(BF16)
        red = _dot(p, ones_bd)
        for hi, h in enumerate(heads):
            lanes = slice(HEAD_DIM * hi, HEAD_DIM * (hi + 1))
            acc_lo = out[h][i][0:8]
            acc_hi = out[h][i][8:SUB]
            n = 0
            for s in range(SUB):
                v_row = _bcast_row(v_s, h, SUB * i + s, 8)
                if s < 8:
                    acc_lo = acc_lo + red[8 * n:8 * n + 8, lanes] * v_row
                    n += 1
                acc_hi = acc_hi + red[8 * n:8 * n + 8, lanes] * v_row
                n += 1
            out[h][i] = jnp.concatenate([acc_lo, acc_hi], axis=0)


def _decay_consts(layer, lb_ref):
    if layer == 0:
        return None
    lb = _layer_lower_bound(lb_ref, layer)
    return jnp.log(lb), jnp.log1p(-lb), 1.0 - lb


def _gate_prep(consts, r0, z_s, q_s, k_s, g_s, b_s, tri):
    rows = slice(r0, r0 + SUB_TILE)
    q_abs = []
    for hd in range(HEADS):
        lanes = slice(HEAD_DIM * hd, HEAD_DIM * (hd + 1))
        fp = z_s[rows, HGRN_WIDTH + HEAD_DIM * hd:HGRN_WIDTH + HEAD_DIM * (hd + 1)]
        e = jnp.exp2(jnp.abs(fp) * NEG_LOG2E)
        log_sig = jnp.minimum(fp, 0.0) - jnp.log(1.0 + e)
        sig_neg = jnp.exp(log_sig - fp)
        if consts is None:
            log_f, kk = log_sig, sig_neg
        else:
            log_lb, log_1m, one_m = consts
            cc = log_1m[:, lanes] + log_sig
            la = log_lb[:, lanes]
            log_f = jnp.maximum(la, cc) + jnp.log(1.0 + jnp.exp2(jnp.abs(la - cc) * NEG_LOG2E))
            kk = one_m[:, lanes] * sig_neg
        k_s[rows, lanes] = kk
        b_s[rows, lanes] = log_f
        q = _silu(z_s[rows, lanes], HEAD_DIM ** -0.5)
        q_s[rows, lanes] = q
        q_abs.append(jnp.max(jnp.abs(q), axis=0, keepdims=True))
        g_s[rows, lanes] = _silu(
            z_s[rows, 3 * HGRN_WIDTH + HEAD_DIM * hd:3 * HGRN_WIDTH + HEAD_DIM * (hd + 1)])

    span = None
    for c in range(SUB_TILE // FCHUNK):
        crows = slice(r0 + FCHUNK * c, r0 + FCHUNK * (c + 1))
        log_f = b_s[crows, :]
        g1 = log_f.astype(BF16)
        r1 = log_f - g1.astype(F32)
        g2 = r1.astype(BF16)
        g3 = (r1 - g2.astype(F32)).astype(BF16)
        b = _dot(tri, g1) + _dot(tri, g2) + _dot(tri, g3)
        b_s[crows, :] = b
        mid = b[FCHUNK // 2 - 1:FCHUNK // 2]
        s = jnp.maximum(b[0:1] - mid, mid - b[FCHUNK - 1:FCHUNK])
        span = s if span is None else jnp.maximum(span, s)
    return span + jnp.maximum(jnp.log(jnp.concatenate(q_abs, axis=1)), 0.0)


def _hgrn_scores(c0, hd, z_s, q_s, k_s, b_s):
    lanes = slice(HEAD_DIM * hd, HEAD_DIM * (hd + 1))
    rows = slice(c0, c0 + FCHUNK)
    b = b_s[rows, lanes]
    mid = b_s[c0 + FCHUNK // 2 - 1:c0 + FCHUNK // 2, lanes]
    last = b_s[c0 + FCHUNK - 1:c0 + FCHUNK, lanes]
    v = z_s[rows, 2 * HGRN_WIDTH + HEAD_DIM * hd:2 * HGRN_WIDTH + HEAD_DIM * (hd + 1)]
    v = v.astype(BF16)
    grow = jnp.exp(b - mid)
    qs = q_s[rows, lanes] * grow
    ks = k_s[rows, lanes] * (1.0 / grow)
    a = _dot_nt(qs.astype(BF16), ks.astype(BF16))
    qe = (qs * jnp.exp(mid)).astype(BF16)
    ke = (ks * jnp.exp(last - mid)).astype(BF16)
    return a, qe, _dot_tn(v, ke), jnp.exp(last)


def _hgrn_outputs(c0, hd, st, scores, z_s, g_s, hg, mix_s, tril):
    a, qe, kv, decay = scores
    lanes = slice(HEAD_DIM * hd, HEAD_DIM * (hd + 1))
    rows = slice(c0, c0 + FCHUNK)
    v = z_s[rows, 2 * HGRN_WIDTH + HEAD_DIM * hd:2 * HGRN_WIDTH + HEAD_DIM * (hd + 1)]
    v = v.astype(BF16)
    a = jnp.where(tril, a, 0.0).astype(BF16)
    o = _dot(a, v) + _dot_nt(qe, st.astype(BF16))
    st = st * decay + kv
    ms = jnp.mean(o * o, axis=-1, keepdims=True)
    on = o * lax.rsqrt(ms + EPS) * hg[:, lanes] * g_s[rows, lanes]
    mix_s[rows, lanes] = on.astype(BF16)
    return st


def _hgrn_any_range(z_s, q_s, k_s, g_s, b_s, hg, st_s, mix_s, cb_s, ck_s, cv_s):
    ones_bd = jnp.where(
        lax.broadcasted_iota(jnp.int32, (2 * HEAD_DIM, 2 * HEAD_DIM), 0) // HEAD_DIM
        == lax.broadcasted_iota(jnp.int32, (2 * HEAD_DIM, 2 * HEAD_DIM), 1) // HEAD_DIM,
        1.0, 0.0).astype(BF16)

    def chunk_body(c, carry):
        for half in range(FCHUNK // CHUNK):
            r0 = pl.multiple_of(c * FCHUNK + CHUNK * half, CHUNK)
            rows = pl.ds(r0, CHUNK)
            b = b_s[rows, :]
            if half:
                b = b - b_s[pl.ds(r0 - 1, 1), :]
            q = q_s[rows, :]
            kk = k_s[rows, :]
            v = z_s[rows, 2 * HGRN_WIDTH:3 * HGRN_WIDTH]
            gate = g_s[rows, :]
            cb_s[...] = jnp.stack([b[:, HEAD_DIM * hd:HEAD_DIM * (hd + 1)] for hd in range(HEADS)])
            ck_s[...] = jnp.stack([kk[:, HEAD_DIM * hd:HEAD_DIM * (hd + 1)] for hd in range(HEADS)])
            cv_s[...] = jnp.stack([v[:, HEAD_DIM * hd:HEAD_DIM * (hd + 1)] for hd in range(HEADS)])
            out = {}
            for hd in range(HEADS):
                lanes = slice(HEAD_DIM * hd, HEAD_DIM * (hd + 1))
                out[hd] = _hgrn_head_offdiag(hd, q[:, lanes], kk[:, lanes], v[:, lanes],
                                             b[:, lanes], st_s, cb_s)
            for pair in ((0, 1), (2, 3)):
                _hgrn_pair_diag(pair,
                                [q[:, HEAD_DIM * hd:HEAD_DIM * (hd + 1)] for hd in pair],
                                [b[:, HEAD_DIM * hd:HEAD_DIM * (hd + 1)] for hd in pair],
                                out, cb_s, ck_s, cv_s, ones_bd)
            for hd in range(HEADS):
                lanes = slice(HEAD_DIM * hd, HEAD_DIM * (hd + 1))
                oh = jnp.concatenate(out[hd], axis=0)
                ms = jnp.mean(oh * oh, axis=-1, keepdims=True)
                on = oh * lax.rsqrt(ms + EPS) * hg[:, lanes] * gate[:, lanes]
                mix_s[rows, lanes] = on.astype(BF16)
        return carry

    lax.fori_loop(0, TQ // FCHUNK, chunk_body, 0)


GATE_GROUPS = (1, 0, 3)
OTHER_GROUPS = (2, 5, 6, 4)


def _mixer_kernel(layer, x_ref, lb_ref, w_in_hbm, w_out_hbm, cw_ref, hg_ref, pre_g_ref,
                  post_g_ref, o_ref, w_in_ref, w_out_ref, w_stage, w_sem, h_s, z_s, q_s, k_s, g_s,
                  b_s, mix_s, cbuf, st_s, st_next_s, cb_s, ck_s, cv_s):
    t = pl.program_id(1)
    subs = [SUB_TILE * i for i in range(TQ // SUB_TILE)]

    @pl.when(_first_grid_step())
    def _():
        _load_weight_bf16(w_in_hbm, layer, w_in_ref, w_stage, w_sem, *w_stage.shape[1:])
        _load_weight_bf16(w_out_hbm, layer, w_out_ref, w_stage, w_sem, *w_stage.shape[1:])

    @pl.when(t == 0)
    def _():
        st_s[...] = jnp.zeros_like(st_s)
        cbuf[:, pl.ds(TQ, HALO), :] = jnp.zeros((CONV_WIDTH // LANES, HALO, LANES), F32)

    cbuf[:, 0:HALO, :] = cbuf[:, TQ:TQ + HALO, :]

    def in_project(r0):
        rows = slice(r0, r0 + SUB_TILE)
        h_s[rows, :] = _rmsnorm(x_ref[rows, :], pre_g_ref[...]).astype(BF16)
        for g in GATE_GROUPS + OTHER_GROUPS:
            cols = slice(512 * g, 512 * (g + 1))
            z_s[rows, cols] = _dot(h_s[rows, :], w_in_ref[:, cols])

    consts = _decay_consts(layer, lb_ref)
    tril = (lax.broadcasted_iota(jnp.int32, (FCHUNK, FCHUNK), 0)
            >= lax.broadcasted_iota(jnp.int32, (FCHUNK, FCHUNK), 1))
    tri = jnp.where(tril, 1.0, 0.0).astype(BF16)
    hg = hg_ref[...]
    cw = cw_ref[...]

    def out_project(r0):
        rows = slice(r0, r0 + SUB_TILE)
        mix = _dot(mix_s[rows, :], w_out_ref[...])
        o_ref[rows, :] = x_ref[rows, :] + _rmsnorm(mix, post_g_ref[...])

    state = [st_s[hd] for hd in range(HEADS)]
    span = None
    in_project(subs[0])
    for i, r0 in enumerate(subs):
        rows = slice(r0, r0 + SUB_TILE)
        s = _gate_prep(consts, r0, z_s, q_s, k_s, g_s, b_s, tri)
        span = s if span is None else jnp.maximum(span, s)
        chunk_rows = [r0 + FCHUNK * c for c in range(SUB_TILE // FCHUNK)]
        scores = {(hd, c0): _hgrn_scores(c0, hd, z_s, q_s, k_s, b_s)
                  for hd in range(HEADS) for c0 in chunk_rows}
        for hd in range(HEADS):
            for c0 in chunk_rows:
                state[hd] = _hgrn_outputs(c0, hd, state[hd], scores[hd, c0], z_s, g_s, hg,
                                          mix_s, tril)
        lo = HALO + r0
        for j in range(CONV_WIDTH // LANES):
            lanes = slice(LANES * j, LANES * (j + 1))
            cbuf[j, lo:lo + SUB_TILE, :] = (z_s[rows, 2560 + LANES * j:2560 + LANES * (j + 1)]
                                            * z_s[rows, 3072 + LANES * j:3072 + LANES * (j + 1)])
            y = (cw[2:3, lanes] * cbuf[j, lo:lo + SUB_TILE, :]
                 + cw[1:2, lanes] * cbuf[j, lo - 1:lo - 1 + SUB_TILE, :]
                 + cw[0:1, lanes] * cbuf[j, lo - 2:lo - 2 + SUB_TILE, :])
            gate_b = z_s[rows, 2048 + LANES * j:2048 + LANES * (j + 1)]
            mix_s[rows, HGRN_WIDTH + LANES * j:HGRN_WIDTH + LANES * (j + 1)] = (
                gate_b * y).astype(BF16)
        if i + 1 < len(subs):
            in_project(subs[i + 1])
        if i >= 1:
            out_project(subs[i - 1])
    out_project(subs[-1])
    for hd in range(HEADS):
        st_next_s[hd] = state[hd]

    @pl.when(jnp.logical_not(jnp.max(span) <= MAX_LOG_SPAN))
    def _():
        st_next_s[...] = st_s[...]
        _hgrn_any_range(z_s, q_s, k_s, g_s, b_s, hg, st_next_s, mix_s, cb_s, ck_s, cv_s)
        for r0 in subs:
            out_project(r0)

    st_s[...] = st_next_s[...]


def _ffn_kernel(layer, x_ref, w_up_hbm, cw_ref, w_down_hbm, pre_g_ref, post_g_ref, o_ref,
                w_up_ref, w_down_ref, up_stage, down_stage, w_sem, ubuf):
    t = pl.program_id(1)

    @pl.when(_first_grid_step())
    def _():
        _load_weight_bf16(w_up_hbm, layer, w_up_ref, up_stage, w_sem, *up_stage.shape[1:])
        _load_weight_bf16(w_down_hbm, layer, w_down_ref, down_stage, w_sem, *down_stage.shape[1:])

    @pl.when(t == 0)
    def _():
        ubuf[:, pl.ds(TQ_FFN, HALO), :] = jnp.zeros((2 * D_FF // LANES, HALO, LANES), F32)

    ubuf[:, 0:HALO, :] = ubuf[:, TQ_FFN:TQ_FFN + HALO, :]
    per_block = FF_BLOCK // LANES
    n_blocks = D_FF // FF_BLOCK

    for sub in range(TQ_FFN // SUB_TILE):
        r0 = SUB_TILE * sub
        x = x_ref[r0:r0 + SUB_TILE, :]
        h = _rmsnorm(x, pre_g_ref[...]).astype(BF16)

        def conv(slab):
            w = cw_ref[:, LANES * slab:LANES * (slab + 1)]
            lo = HALO + r0
            return (w[2:3] * ubuf[slab, lo:lo + SUB_TILE, :]
                    + w[1:2] * ubuf[slab, lo - 1:lo - 1 + SUB_TILE, :]
                    + w[0:1] * ubuf[slab, lo - 2:lo - 2 + SUB_TILE, :])

        def up_project(g):
            for base in (FF_BLOCK * g, D_FF + FF_BLOCK * g):
                u = _dot(h, w_up_ref[:, base:base + FF_BLOCK])
                for j in range(per_block):
                    ubuf[base // LANES + j, HALO + r0:HALO + r0 + SUB_TILE, :] = (
                        u[:, LANES * j:LANES * (j + 1)])

        def activate(g):
            act = []
            for j in range(per_block):
                gate = conv(FF_BLOCK * g // LANES + j)
                val = conv((D_FF + FF_BLOCK * g) // LANES + j)
                act.append((gate * _sigmoid(gate) * val).astype(BF16))
            return jnp.concatenate(act, axis=1)

        def down_project(y, g, act):
            part = _dot(act, w_down_ref[FF_BLOCK * g:FF_BLOCK * (g + 1), :])
            return part if y is None else y + part

        y = None
        prev_act = None
        up_project(0)
        for g in range(n_blocks):
            if g + 1 < n_blocks:
                up_project(g + 1)
            if g >= 1:
                y = down_project(y, g - 1, prev_act)
            prev_act = activate(g)
        y = down_project(y, n_blocks - 1, prev_act)

        o_ref[r0:r0 + SUB_TILE, :] = x + _rmsnorm(y, post_g_ref[...])


def _x_spec(tq):
    return pl.BlockSpec((None, tq, D_MODEL), lambda b, t: (b, t, 0))


def _layer_spec(shape, layer):
    return pl.BlockSpec((None,) + shape, lambda b, t: (layer,) + (0,) * len(shape),
                        pipeline_mode=pl.Buffered(1))


_HBM_SPEC = pl.BlockSpec(memory_space=pl.ANY)


_COMPILER_PARAMS = pltpu.CompilerParams(
    dimension_semantics=("arbitrary", "arbitrary"), vmem_limit_bytes=VMEM_LIMIT_BYTES)


def _mixer_call(layer, x, lb_param, w_in, w_out, conv_w, hgrn_g, pre_g, post_g):
    batch, seq, _ = x.shape
    return pl.pallas_call(
        functools.partial(_mixer_kernel, layer),
        out_shape=jax.ShapeDtypeStruct(x.shape, x.dtype),
        grid=(batch, seq // TQ),
        in_specs=[
            _x_spec(TQ),
            pl.BlockSpec((DEPTH, HGRN_WIDTH), lambda b, t: (0, 0)),
            _HBM_SPEC,
            _HBM_SPEC,
            _layer_spec((3, CONV_WIDTH), layer),
            _layer_spec((1, HGRN_WIDTH), layer),
            _layer_spec((1, D_MODEL), layer),
            _layer_spec((1, D_MODEL), layer),
        ],
        out_specs=_x_spec(TQ),
        scratch_shapes=[
            pltpu.VMEM((D_MODEL, IN_COLS), BF16),
            pltpu.VMEM((D_MODEL, D_MODEL), BF16),
            pltpu.VMEM((W_SLOTS,) + W_STAGE, F32),
            pltpu.SemaphoreType.DMA((W_SLOTS,)),
            pltpu.VMEM((TQ, D_MODEL), BF16),
            pltpu.VMEM((TQ, IN_COLS), F32),
            pltpu.VMEM((TQ, HGRN_WIDTH), F32),
            pltpu.VMEM((TQ, HGRN_WIDTH), F32),
            pltpu.VMEM((TQ, HGRN_WIDTH), F32),
            pltpu.VMEM((TQ, HGRN_WIDTH), F32),
            pltpu.VMEM((TQ, D_MODEL), BF16),
            pltpu.VMEM((CONV_WIDTH // LANES, TQ + HALO, LANES), F32),
            pltpu.VMEM((HEADS, HEAD_DIM, HEAD_DIM), F32),
            pltpu.VMEM((HEADS, HEAD_DIM, HEAD_DIM), F32),
            pltpu.VMEM((HEADS, CHUNK, HEAD_DIM), F32),
            pltpu.VMEM((HEADS, CHUNK, HEAD_DIM), F32),
            pltpu.VMEM((HEADS, CHUNK, HEAD_DIM), F32),
        ],
        compiler_params=_COMPILER_PARAMS,
        name=f"mixer_l{layer}",
    )(x, lb_param, w_in, w_out, conv_w, hgrn_g, pre_g, post_g)


def _ffn_call(layer, x, w_up, conv_w, w_down, pre_g, post_g):
    batch, seq, _ = x.shape
    return pl.pallas_call(
        functools.partial(_ffn_kernel, layer),
        out_shape=jax.ShapeDtypeStruct(x.shape, x.dtype),
        grid=(batch, seq // TQ_FFN),
        in_specs=[
            _x_spec(TQ_FFN),
            _HBM_SPEC,
            _layer_spec((3, 2 * D_FF), layer),
            _HBM_SPEC,
            _layer_spec((1, D_MODEL), layer),
            _layer_spec((1, D_MODEL), layer),
        ],
        out_specs=_x_spec(TQ_FFN),
        scratch_shapes=[
            pltpu.VMEM((D_MODEL, 2 * D_FF), BF16),
            pltpu.VMEM((D_FF, D_MODEL), BF16),
            pltpu.VMEM((W_SLOTS,) + W_STAGE, F32),
            pltpu.VMEM((W_SLOTS,) + W_STAGE_DOWN, F32),
            pltpu.SemaphoreType.DMA((W_SLOTS,)),
            pltpu.VMEM((2 * D_FF // LANES, TQ_FFN + HALO, LANES), F32),
        ],
        compiler_params=_COMPILER_PARAMS,
        name=f"ffn_l{layer}",
    )(x, w_up, conv_w, w_down, pre_g, post_g)


def kernel(x, lb_param, w_in, w_out, conv_w, ffn_w_up, ffn_conv_w, ffn_w_down, hgrn_norm_g,
           pre_mix_g, post_mix_g, pre_ffn_g, post_ffn_g):
    assert x.shape[1] % TQ == 0 and x.shape[1] % TQ_FFN == 0 and x.shape[2] == D_MODEL
    hgrn_g = hgrn_norm_g[:, None, :]
    pre_mix, post_mix = pre_mix_g[:, None, :], post_mix_g[:, None, :]
    pre_ffn, post_ffn = pre_ffn_g[:, None, :], post_ffn_g[:, None, :]
    for layer in range(DEPTH):
        x = _mixer_call(layer, x, lb_param, w_in, w_out, conv_w, hgrn_g, pre_mix, post_mix)
        x = _ffn_call(layer, x, ffn_w_up, ffn_conv_w, ffn_w_down, pre_ffn, post_ffn)
    return x
```

```python
import functools

import jax
import jax.numpy as jnp
from jax import lax
from jax.experimental import pallas as pl
from jax.experimental.pallas import tpu as pltpu

D_MODEL = 1024
DEPTH = 4
HGRN_WIDTH = 512
HEADS = 4
HEAD_DIM = 128
CONV_WIDTH = 512
IN_COLS = 4 * HGRN_WIDTH + 3 * CONV_WIDTH
D_FF = 2816
EPS = 1e-6

TQ = 512
TQ_FFN = 512
SUB_TILE = 256
FCHUNK = 128
MAX_LOG_SPAN = 80.0
CHUNK = 64
SUB = 16
NSUB = CHUNK // SUB
HALO = 8
LANES = 128
FF_BLOCK = 256
NEG_BIG = -1e30
NEG_LOG2E = -1.4426950408889634
W_SLOTS = 3
W_STAGE = (1024, 512)
W_STAGE_DOWN = (256, 1024)
VMEM_LIMIT_BYTES = 56 * 1024 * 1024

F32 = jnp.float32
BF16 = jnp.bfloat16


def _dot(a, b):
    return jnp.dot(a, b, preferred_element_type=F32)


def _dot_nt(a, b):
    return lax.dot_general(a, b, (((1,), (1,)), ((), ())), preferred_element_type=F32)


def _dot_tn(a, b):
    return lax.dot_general(a, b, (((0,), (0,)), ((), ())), preferred_element_type=F32)


def _rmsnorm(x, g):
    ms = jnp.mean(x * x, axis=-1, keepdims=True)
    return x * lax.rsqrt(ms + EPS) * g


def _sigmoid(x):
    return 1.0 / (1.0 + jnp.exp(-x))


def _silu(x, scale=1.0):
    half = x * (0.5 * scale)
    return half + half * jnp.tanh(0.5 * x)


def _first_grid_step():
    return jnp.logical_and(pl.program_id(0) == 0, pl.program_id(1) == 0)


def _load_weight_bf16(w_hbm, layer, w_bf, stage, sem, row_chunk, col_chunk):
    rows, cols = w_bf.shape
    n_slots = stage.shape[0]
    pieces = [(row_chunk * r, col_chunk * c)
              for r in range(rows // row_chunk) for c in range(cols // col_chunk)]

    def copy(i):
        r0, c0 = pieces[i]
        src = w_hbm.at[layer, pl.ds(r0, row_chunk), pl.ds(c0, col_chunk)]
        return pltpu.make_async_copy(src, stage.at[i % n_slots], sem.at[i % n_slots])

    for i in range(min(n_slots, len(pieces))):
        copy(i).start()
    for i, (r0, c0) in enumerate(pieces):
        copy(i).wait()
        w_bf[r0:r0 + row_chunk, c0:c0 + col_chunk] = stage[i % n_slots].astype(BF16)
        if i + n_slots < len(pieces):
            copy(i + n_slots).start()


def _layer_lower_bound(lb_ref, layer):
    lbp = lb_ref[...]
    m = jnp.max(lbp, axis=0, keepdims=True)
    e = jnp.exp(lbp - m)
    p = e / jnp.sum(e, axis=0, keepdims=True)
    lb = p[1:2]
    for j in range(2, layer + 1):
        lb = lb + p[j:j + 1]
    return lb


def _bcast_row(ref, head, row, nrows):
    return jnp.broadcast_to(ref[head, pl.ds(row, 1), :], (nrows, HEAD_DIM))


def _hgrn_head_offdiag(head, q, k, v, b, st_ref, b_s):
    blk = [slice(SUB * i, SUB * (i + 1)) for i in range(NSUB)]
    bend = [_bcast_row(b_s, head, SUB * i + SUB - 1, SUB) for i in range(NSUB)]
    qp = [q[blk[0]] * jnp.exp(b[blk[0]])]
    for i in range(1, NSUB):
        qp.append(q[blk[i]] * jnp.exp(b[blk[i]] - bend[i - 1]))
    kp = [k[blk[i]] * jnp.exp(bend[i] - b[blk[i]]) for i in range(NSUB)]

    st = st_ref[head]
    st_b = st.astype(BF16)
    qb = [qp[0]] + [qp[i] * jnp.exp(bend[i - 1]) for i in range(1, NSUB)]
    o_inter = _dot_nt(jnp.concatenate(qb, axis=0).astype(BF16), st_b)
    out = [o_inter[blk[i]] for i in range(NSUB)]

    v_b = v.astype(BF16)
    for i in range(1, NSUB):
        parts = [kp[j] * jnp.exp(bend[i - 1] - bend[j]) for j in range(i - 1)] + [kp[i - 1]]
        kcat = jnp.concatenate(parts, axis=0) if len(parts) > 1 else parts[0]
        a = _dot_nt(qp[i].astype(BF16), kcat.astype(BF16))
        out[i] = out[i] + _dot(a.astype(BF16), v_b[0:SUB * i])

    last = NSUB - 1
    kend = jnp.concatenate(
        [kp[j] * jnp.exp(bend[last] - bend[j]) for j in range(last)] + [kp[last]], axis=0)
    decay = jnp.exp(_bcast_row(b_s, head, CHUNK - 1, HEAD_DIM))
    st_ref[head] = st * decay + _dot_tn(v_b, kend.astype(BF16))
    return out


def _hgrn_pair_diag(heads, qs, bs, out, b_s, k_s, v_s, ones_bd):
    row = lax.broadcasted_iota(jnp.int32, (8, HEAD_DIM), 0)
    for i in range(NSUB):
        pieces = {h: [] for h in heads}
        for h, q, b in zip(heads, qs, bs):
            q_lo, q_hi = q[SUB * i:SUB * i + 8], q[SUB * i + 8:SUB * (i + 1)]
            b_lo, b_hi = b[SUB * i:SUB * i + 8], b[SUB * i + 8:SUB * (i + 1)]
            for s in range(SUB):
                r = SUB * i + s
                b_row = _bcast_row(b_s, h, r, 8)
                k_row = _bcast_row(k_s, h, r, 8)
                if s < 8:
                    d_lo = jnp.where(row >= s, b_lo - b_row, NEG_BIG)
                    pieces[h].append(q_lo * jnp.exp(d_lo) * k_row)
                    pieces[h].append(q_hi * jnp.exp(b_hi - b_row) * k_row)
                else:
                    d_hi = jnp.where(row >= s - 8, b_hi - b_row, NEG_BIG)
                    pieces[h].append(q_hi * jnp.exp(d_hi) * k_row)
        p = jnp.concatenate(
            [jnp.concatenate(pieces[h], axis=0) for h in heads], axis=1).astype(BF16)
        red = _dot(p, ones_bd)
        for hi, h in enumerate(heads):
            lanes = slice(HEAD_DIM * hi, HEAD_DIM * (hi + 1))
            acc_lo = out[h][i][0:8]
            acc_hi = out[h][i][8:SUB]
            n = 0
            for s in range(SUB):
                v_row = _bcast_row(v_s, h, SUB * i + s, 8)
                if s < 8:
                    acc_lo = acc_lo + red[8 * n:8 * n + 8, lanes] * v_row
                    n += 1
                acc_hi = acc_hi + red[8 * n:8 * n + 8, lanes] * v_row
                n += 1
            out[h][i] = jnp.concatenate([acc_lo, acc_hi], axis=0)


def _decay_consts(layer, lb_ref):
    if layer == 0:
        return None
    lb = _layer_lower_bound(lb_ref, layer)
    return jnp.log(lb), jnp.log1p(-lb), 1.0 - lb


def _gate_prep(consts, r0, z_s, q_s, k_s, g_s, b_s, tri):
    rows = slice(r0, r0 + SUB_TILE)
    q_abs = []
    for hd in range(HEADS):
        lanes = slice(HEAD_DIM * hd, HEAD_DIM * (hd + 1))
        fp = z_s[rows, HGRN_WIDTH + HEAD_DIM * hd:HGRN_WIDTH + HEAD_DIM * (hd + 1)]
        e = jnp.exp2(jnp.abs(fp) * NEG_LOG2E)
        log_sig = jnp.minimum(fp, 0.0) - jnp.log(1.0 + e)
        sig_neg = jnp.exp(log_sig - fp)
        if consts is None:
            log_f, kk = log_sig, sig_neg
        else:
            log_lb, log_1m, one_m = consts
            cc = log_1m[:, lanes] + log_sig
            la = log_lb[:, lanes]
            log_f = jnp.maximum(la, cc) + jnp.log(1.0 + jnp.exp2(jnp.abs(la - cc) * NEG_LOG2E))
            kk = one_m[:, lanes] * sig_neg
        k_s[rows, lanes] = kk
        b_s[rows, lanes] = log_f
        q = _silu(z_s[rows, lanes], HEAD_DIM ** -0.5)
        q_s[rows, lanes] = q
        q_abs.append(jnp.max(jnp.abs(q), axis=0, keepdims=True))
        g_s[rows, lanes] = _silu(
            z_s[rows, 3 * HGRN_WIDTH + HEAD_DIM * hd:3 * HGRN_WIDTH + HEAD_DIM * (hd + 1)])

    span = None
    for c in range(SUB_TILE // FCHUNK):
        crows = slice(r0 + FCHUNK * c, r0 + FCHUNK * (c + 1))
        log_f = b_s[crows, :]
        g1 = log_f.astype(BF16)
        r1 = log_f - g1.astype(F32)
        g2 = r1.astype(BF16)
        g3 = (r1 - g2.astype(F32)).astype(BF16)
        b = _dot(tri, g1) + _dot(tri, g2) + _dot(tri, g3)
        b_s[crows, :] = b
        mid = b[FCHUNK // 2 - 1:FCHUNK // 2]
        s = jnp.maximum(b[0:1] - mid, mid - b[FCHUNK - 1:FCHUNK])
        span = s if span is None else jnp.maximum(span, s)
    return span + jnp.maximum(jnp.log(jnp.concatenate(q_abs, axis=1)), 0.0)


def _hgrn_scores(c0, hd, z_s, q_s, k_s, b_s):
    lanes = slice(HEAD_DIM * hd, HEAD_DIM * (hd + 1))
    rows = slice(c0, c0 + FCHUNK)
    b = b_s[rows, lanes]
    mid = b_s[c0 + FCHUNK // 2 - 1:c0 + FCHUNK // 2, lanes]
    last = b_s[c0 + FCHUNK - 1:c0 + FCHUNK, lanes]
    v = z_s[rows, 2 * HGRN_WIDTH + HEAD_DIM * hd:2 * HGRN_WIDTH + HEAD_DIM * (hd + 1)]
    v = v.astype(BF16)
    grow = jnp.exp(b - mid)
    qs = q_s[rows, lanes] * grow
    ks = k_s[rows, lanes] * (1.0 / grow)
    a = _dot(qs.astype(BF16), ks.T.astype(BF16))
    qe = (qs * jnp.exp(mid)).astype(BF16)
    ke = (ks * jnp.exp(last - mid)).astype(BF16)
    return a, qe, _dot_tn(v, ke), jnp.exp(last)


def _hgrn_outputs(c0, hd, st, scores, z_s, g_s, hg, mix_s, tril):
    a, qe, kv, decay = scores
    lanes = slice(HEAD_DIM * hd, HEAD_DIM * (hd + 1))
    rows = slice(c0, c0 + FCHUNK)
    v = z_s[rows, 2 * HGRN_WIDTH + HEAD_DIM * hd:2 * HGRN_WIDTH + HEAD_DIM * (hd + 1)]
    v = v.astype(BF16)
    a = jnp.where(tril, a, 0.0).astype(BF16)
    o = _dot(a, v) + _dot(qe, st.T.astype(BF16))
    st = st * decay + kv
    ms = jnp.mean(o * o, axis=-1, keepdims=True)
    on = o * lax.rsqrt(ms + EPS) * hg[:, lanes] * g_s[rows, lanes]
    mix_s[rows, lanes] = on.astype(BF16)
    return st


def _hgrn_any_range(z_s, q_s, k_s, g_s, b_s, hg, st_s, mix_s, cb_s, ck_s, cv_s):
    ones_bd = jnp.where(
        lax.broadcasted_iota(jnp.int32, (2 * HEAD_DIM, 2 * HEAD_DIM), 0) // HEAD_DIM
        == lax.broadcasted_iota(jnp.int32, (2 * HEAD_DIM, 2 * HEAD_DIM), 1) // HEAD_DIM,
        1.0, 0.0).astype(BF16)

    def chunk_body(c, carry):
        for half in range(FCHUNK // CHUNK):
            r0 = pl.multiple_of(c * FCHUNK + CHUNK * half, CHUNK)
            rows = pl.ds(r0, CHUNK)
            b = b_s[rows, :]
            if half:
                b = b - b_s[pl.ds(r0 - 1, 1), :]
            q = q_s[rows, :]
            kk = k_s[rows, :]
            v = z_s[rows, 2 * HGRN_WIDTH:3 * HGRN_WIDTH]
            gate = g_s[rows, :]
            cb_s[...] = jnp.stack([b[:, HEAD_DIM * hd:HEAD_DIM * (hd + 1)] for hd in range(HEADS)])
            ck_s[...] = jnp.stack([kk[:, HEAD_DIM * hd:HEAD_DIM * (hd + 1)] for hd in range(HEADS)])
            cv_s[...] = jnp.stack([v[:, HEAD_DIM * hd:HEAD_DIM * (hd + 1)] for hd in range(HEADS)])
            out = {}
            for hd in range(HEADS):
                lanes = slice(HEAD_DIM * hd, HEAD_DIM * (hd + 1))
                out[hd] = _hgrn_head_offdiag(hd, q[:, lanes], kk[:, lanes], v[:, lanes],
                                             b[:, lanes], st_s, cb_s)
            for pair in ((0, 1), (2, 3)):
                _hgrn_pair_diag(pair,
                                [q[:, HEAD_DIM * hd:HEAD_DIM * (hd + 1)] for hd in pair],
                                [b[:, HEAD_DIM * hd:HEAD_DIM * (hd + 1)] for hd in pair],
                                out, cb_s, ck_s, cv_s, ones_bd)
            for hd in range(HEADS):
                lanes = slice(HEAD_DIM * hd, HEAD_DIM * (hd + 1))
                oh = jnp.concatenate(out[hd], axis=0)
                ms = jnp.mean(oh * oh, axis=-1, keepdims=True)
                on = oh * lax.rsqrt(ms + EPS) * hg[:, lanes] * gate[:, lanes]
                mix_s[rows, lanes] = on.astype(BF16)
        return carry

    lax.fori_loop(0, TQ // FCHUNK, chunk_body, 0)


GATE_GROUPS = (1, 0, 3)
OTHER_GROUPS = (2, 5, 6, 4)


def _mixer_kernel(layer, x_ref, lb_ref, w_in_hbm, w_out_hbm, cw_ref, hg_ref, pre_g_ref,
                  post_g_ref, o_ref, w_in_ref, w_out_ref, w_stage, w_sem, h_s, z_s, q_s, k_s, g_s,
                  b_s, mix_s, cbuf, st_s, st_next_s, cb_s, ck_s, cv_s):
    t = pl.program_id(1)
    subs = [SUB_TILE * i for i in range(TQ // SUB_TILE)]

    @pl.when(_first_grid_step())
    def _():
        _load_weight_bf16(w_in_hbm, layer, w_in_ref, w_stage, w_sem, *w_stage.shape[1:])
        _load_weight_bf16(w_out_hbm, layer, w_out_ref, w_stage, w_sem, *w_stage.shape[1:])

    @pl.when(t == 0)
    def _():
        st_s[...] = jnp.zeros_like(st_s)
        cbuf[:, pl.ds(TQ, HALO), :] = jnp.zeros((CONV_WIDTH // LANES, HALO, LANES), F32)

    cbuf[:, 0:HALO, :] = cbuf[:, TQ:TQ + HALO, :]

    def in_project(r0):
        rows = slice(r0, r0 + SUB_TILE)
        h_s[rows, :] = _rmsnorm(x_ref[rows, :], pre_g_ref[...]).astype(BF16)
        for g in GATE_GROUPS + OTHER_GROUPS:
            cols = slice(512 * g, 512 * (g + 1))
            z_s[rows, cols] = _dot(h_s[rows, :], w_in_ref[:, cols])

    consts = _decay_consts(layer, lb_ref)
    tril = (lax.broadcasted_iota(jnp.int32, (FCHUNK, FCHUNK), 0)
            >= lax.broadcasted_iota(jnp.int32, (FCHUNK, FCHUNK), 1))
    tri = jnp.where(tril, 1.0, 0.0).astype(BF16)
    hg = hg_ref[...]
    cw = cw_ref[...]

    def out_project(r0):
        rows = slice(r0, r0 + SUB_TILE)
        mix = _dot(mix_s[rows, :], w_out_ref[...])
        o_ref[rows, :] = x_ref[rows, :] + _rmsnorm(mix, post_g_ref[...])

    state = [st_s[hd] for hd in range(HEADS)]
    span = None
    in_project(subs[0])
    for i, r0 in enumerate(subs):
        rows = slice(r0, r0 + SUB_TILE)
        s = _gate_prep(consts, r0, z_s, q_s, k_s, g_s, b_s, tri)
        span = s if span is None else jnp.maximum(span, s)
        chunk_rows = [r0 + FCHUNK * c for c in range(SUB_TILE // FCHUNK)]
        scores = {(hd, c0): _hgrn_scores(c0, hd, z_s, q_s, k_s, b_s)
                  for hd in range(HEADS) for c0 in chunk_rows}
        for hd in range(HEADS):
            for c0 in chunk_rows:
                state[hd] = _hgrn_outputs(c0, hd, state[hd], scores[hd, c0], z_s, g_s, hg,
                                          mix_s, tril)
        lo = HALO + r0
        for j in range(CONV_WIDTH // LANES):
            lanes = slice(LANES * j, LANES * (j + 1))
            cbuf[j, lo:lo + SUB_TILE, :] = (z_s[rows, 2560 + LANES * j:2560 + LANES * (j + 1)]
                                            * z_s[rows, 3072 + LANES * j:3072 + LANES * (j + 1)])
            y = (cw[2:3, lanes] * cbuf[j, lo:lo + SUB_TILE, :]
                 + cw[1:2, lanes] * cbuf[j, lo - 1:lo - 1 + SUB_TILE, :]
                 + cw[0:1, lanes] * cbuf[j, lo - 2:lo - 2 + SUB_TILE, :])
            gate_b = z_s[rows, 2048 + LANES * j:2048 + LANES * (j + 1)]
            mix_s[rows, HGRN_WIDTH + LANES * j:HGRN_WIDTH + LANES * (j + 1)] = (
                gate_b * y).astype(BF16)
        if i + 1 < len(subs):
            in_project(subs[i + 1])
        if i >= 1:
            out_project(subs[i - 1])
    out_project(subs[-1])
    for hd in range(HEADS):
        st_next_s[hd] = state[hd]

    @pl.when(jnp.logical_not(jnp.max(span) <= MAX_LOG_SPAN))
    def _():
        st_next_s[...] = st_s[...]
        _hgrn_any_range(z_s, q_s, k_s, g_s, b_s, hg, st_next_s, mix_s, cb_s, ck_s, cv_s)
        for r0 in subs:
            out_project(r0)

    st_s[...] = st_next_s[...]


def _ffn_kernel(layer, x_ref, w_up_hbm, cw_ref, w_down_hbm, pre_g_ref, post_g_ref, o_ref,
                w_up_ref, w_down_ref, up_stage, down_stage, w_sem, ubuf):
    t = pl.program_id(1)

    @pl.when(_first_grid_step())
    def _():
        _load_weight_bf16(w_up_hbm, layer, w_up_ref, up_stage, w_sem, *up_stage.shape[1:])
        _load_weight_bf16(w_down_hbm, layer, w_down_ref, down_stage, w_sem, *down_stage.shape[1:])

    @pl.when(t == 0)
    def _():
        ubuf[:, pl.ds(TQ_FFN, HALO), :] = jnp.zeros((2 * D_FF // LANES, HALO, LANES), F32)

    ubuf[:, 0:HALO, :] = ubuf[:, TQ_FFN:TQ_FFN + HALO, :]
    per_block = FF_BLOCK // LANES
    n_blocks = D_FF // FF_BLOCK

    for sub in range(TQ_FFN // SUB_TILE):
        r0 = SUB_TILE * sub
        x = x_ref[r0:r0 + SUB_TILE, :]
        h = _rmsnorm(x, pre_g_ref[...]).astype(BF16)

        def conv(slab):
            w = cw_ref[:, LANES * slab:LANES * (slab + 1)]
            lo = HALO + r0
            return (w[2:3] * ubuf[slab, lo:lo + SUB_TILE, :]
                    + w[1:2] * ubuf[slab, lo - 1:lo - 1 + SUB_TILE, :]
                    + w[0:1] * ubuf[slab, lo - 2:lo - 2 + SUB_TILE, :])

        def up_project(g):
            for base in (FF_BLOCK * g, D_FF + FF_BLOCK * g):
                u = _dot(h, w_up_ref[:, base:base + FF_BLOCK])
                for j in range(per_block):
                    ubuf[base // LANES + j, HALO + r0:HALO + r0 + SUB_TILE, :] = (
                        u[:, LANES * j:LANES * (j + 1)])

        def activate(g):
            act = []
            for j in range(per_block):
                gate = conv(FF_BLOCK * g // LANES + j)
                val = conv((D_FF + FF_BLOCK * g) // LANES + j)
                act.append((gate * _sigmoid(gate) * val).astype(BF16))
            return jnp.concatenate(act, axis=1)

        def down_project(y, g, act):
            part = _dot(act, w_down_ref[FF_BLOCK * g:FF_BLOCK * (g + 1), :])
            return part if y is None else y + part

        y = None
        prev_act = None
        up_project(0)
        for g in range(n_blocks):
            if g + 1 < n_blocks:
                up_project(g + 1)
            if g >= 1:
                y = down_project(y, g - 1, prev_act)
            prev_act = activate(g)
        y = down_project(y, n_blocks - 1, prev_act)

        o_ref[r0:r0 + SUB_TILE, :] = x + _rmsnorm(y, post_g_ref[...])


def _x_spec(tq):
    return pl.BlockSpec((None, tq, D_MODEL), lambda b, t: (b, t, 0))


def _layer_spec(shape, layer):
    return pl.BlockSpec((None,) + shape, lambda b, t: (layer,) + (0,) * len(shape),
                        pipeline_mode=pl.Buffered(1))


_HBM_SPEC = pl.BlockSpec(memory_space=pl.ANY)


_COMPILER_PARAMS = pltpu.CompilerParams(
    dimension_semantics=("arbitrary", "arbitrary"), vmem_limit_bytes=VMEM_LIMIT_BYTES)


def _mixer_call(layer, x, lb_param, w_in, w_out, conv_w, hgrn_g, pre_g, post_g):
    batch, seq, _ = x.shape
    return pl.pallas_call(
        functools.partial(_mixer_kernel, layer),
        out_shape=jax.ShapeDtypeStruct(x.shape, x.dtype),
        grid=(batch, seq // TQ),
        in_specs=[
            _x_spec(TQ),
            pl.BlockSpec((DEPTH, HGRN_WIDTH), lambda b, t: (0, 0)),
            _HBM_SPEC,
            _HBM_SPEC,
            _layer_spec((3, CONV_WIDTH), layer),
            _layer_spec((1, HGRN_WIDTH), layer),
            _layer_spec((1, D_MODEL), layer),
            _layer_spec((1, D_MODEL), layer),
        ],
        out_specs=_x_spec(TQ),
        scratch_shapes=[
            pltpu.VMEM((D_MODEL, IN_COLS), BF16),
            pltpu.VMEM((D_MODEL, D_MODEL), BF16),
            pltpu.VMEM((W_SLOTS,) + W_STAGE, F32),
            pltpu.SemaphoreType.DMA((W_SLOTS,)),
            pltpu.VMEM((TQ, D_MODEL), BF16),
            pltpu.VMEM((TQ, IN_COLS), F32),
            pltpu.VMEM((TQ, HGRN_WIDTH), F32),
            pltpu.VMEM((TQ, HGRN_WIDTH), F32),
            pltpu.VMEM((TQ, HGRN_WIDTH), F32),
            pltpu.VMEM((TQ, HGRN_WIDTH), F32),
            pltpu.VMEM((TQ, D_MODEL), BF16),
            pltpu.VMEM((CONV_WIDTH // LANES, TQ + HALO, LANES), F32),
            pltpu.VMEM((HEADS, HEAD_DIM, HEAD_DIM), F32),
            pltpu.VMEM((HEADS, HEAD_DIM, HEAD_DIM), F32),
            pltpu.VMEM((HEADS, CHUNK, HEAD_DIM), F32),
            pltpu.VMEM((HEADS, CHUNK, HEAD_DIM), F32),
            pltpu.VMEM((HEADS, CHUNK, HEAD_DIM), F32),
        ],
        compiler_params=_COMPILER_PARAMS,
        name=f"mixer_l{layer}",
    )(x, lb_param, w_in, w_out, conv_w, hgrn_g, pre_g, post_g)


def _ffn_call(layer, x, w_up, conv_w, w_down, pre_g, post_g):
    batch, seq, _ = x.shape
    return pl.pallas_call(
        functools.partial(_ffn_kernel, layer),
        out_shape=jax.ShapeDtypeStruct(x.shape, x.dtype),
        grid=(batch, seq // TQ_FFN),
        in_specs=[
            _x_spec(TQ_FFN),
            _HBM_SPEC,
            _layer_spec((3, 2 * D_FF), layer),
            _HBM_SPEC,
            _layer_spec((1, D_MODEL), layer),
            _layer_spec((1, D_MODEL), layer),
        ],
        out_specs=_x_spec(TQ_FFN),
        scratch_shapes=[
            pltpu.VMEM((D_MODEL, 2 * D_FF), BF16),
            pltpu.VMEM((D_FF, D_MODEL), BF16),
            pltpu.VMEM((W_SLOTS,) + W_STAGE, F32),
            pltpu.VMEM((W_SLOTS,) + W_STAGE_DOWN, F32),
            pltpu.SemaphoreType.DMA((W_SLOTS,)),
            pltpu.VMEM((2 * D_FF // LANES, TQ_FFN + HALO, LANES), F32),
        ],
        compiler_params=_COMPILER_PARAMS,
        name=f"ffn_l{layer}",
    )(x, w_up, conv_w, w_down, pre_g, post_g)


def kernel(x, lb_param, w_in, w_out, conv_w, ffn_w_up, ffn_conv_w, ffn_w_down, hgrn_norm_g,
           pre_mix_g, post_mix_g, pre_ffn_g, post_ffn_g):
    assert x.shape[1] % TQ == 0 and x.shape[1] % TQ_FFN == 0 and x.shape[2] == D_MODEL
    hgrn_g = hgrn_norm_g[:, None, :]
    pre_mix, post_mix = pre_mix_g[:, None, :], post_mix_g[:, None, :]
    pre_ffn, post_ffn = pre_ffn_g[:, None, :], post_ffn_g[:, None, :]
    for layer in range(DEPTH):
        x = _mixer_call(layer, x, lb_param, w_in, w_out, conv_w, hgrn_g, pre_mix, post_mix)
        x = _ffn_call(layer, x, ffn_w_up, ffn_conv_w, ffn_w_down, pre_ffn, post_ffn)
    return x
```

```python
import functools

import jax
import jax.numpy as jnp
from jax import lax
from jax.experimental import pallas as pl
from jax.experimental.pallas import tpu as pltpu

D_MODEL = 1024
DEPTH = 4
HGRN_WIDTH = 512
HEADS = 4
HEAD_DIM = 128
CONV_WIDTH = 512
IN_COLS = 4 * HGRN_WIDTH + 3 * CONV_WIDTH
D_FF = 2816
EPS = 1e-6

TQ = 512
TQ_FFN = 512
SUB_TILE = 256
FCHUNK = 128
MAX_LOG_SPAN = 80.0
CHUNK = 64
SUB = 16
NSUB = CHUNK // SUB
HALO = 8
LANES = 128
FF_BLOCK = 256
NEG_BIG = -1e30
NEG_LOG2E = -1.4426950408889634
W_SLOTS = 3
W_STAGE = (1024, 512)
W_STAGE_DOWN = (256, 1024)
VMEM_LIMIT_BYTES = 56 * 1024 * 1024

F32 = jnp.float32
BF16 = jnp.bfloat16


def _dot(a, b):
    return jnp.dot(a, b, preferred_element_type=F32)


def _dot_nt(a, b):
    return lax.dot_general(a, b, (((1,), (1,)), ((), ())), preferred_element_type=F32)


def _dot_tn(a, b):
    return lax.dot_general(a, b, (((0,), (0,)), ((), ())), preferred_element_type=F32)


def _rmsnorm(x, g):
    ms = jnp.mean(x * x, axis=-1, keepdims=True)
    return x * lax.rsqrt(ms + EPS) * g


def _sigmoid(x):
    return 1.0 / (1.0 + jnp.exp(-x))


def _silu(x, scale=1.0):
    half = x * (0.5 * scale)
    return half + half * jnp.tanh(0.5 * x)


def _first_grid_step():
    return jnp.logical_and(pl.program_id(0) == 0, pl.program_id(1) == 0)


def _load_weight_bf16(w_hbm, layer, w_bf, stage, sem, row_chunk, col_chunk):
    rows, cols = w_bf.shape
    n_slots = stage.shape[0]
    pieces = [(row_chunk * r, col_chunk * c)
              for r in range(rows // row_chunk) for c in range(cols // col_chunk)]

    def copy(i):
        r0, c0 = pieces[i]
        src = w_hbm.at[layer, pl.ds(r0, row_chunk), pl.ds(c0, col_chunk)]
        return pltpu.make_async_copy(src, stage.at[i % n_slots], sem.at[i % n_slots])

    for i in range(min(n_slots, len(pieces))):
        copy(i).start()
    for i, (r0, c0) in enumerate(pieces):
        copy(i).wait()
        w_bf[r0:r0 + row_chunk, c0:c0 + col_chunk] = stage[i % n_slots].astype(BF16)
        if i + n_slots < len(pieces):
            copy(i + n_slots).start()


def _layer_lower_bound(lb_ref, layer):
    lbp = lb_ref[...]
    m = jnp.max(lbp, axis=0, keepdims=True)
    e = jnp.exp(lbp - m)
    p = e / jnp.sum(e, axis=0, keepdims=True)
    lb = p[1:2]
    for j in range(2, layer + 1):
        lb = lb + p[j:j + 1]
    return lb


def _bcast_row(ref, head, row, nrows):
    return jnp.broadcast_to(ref[head, pl.ds(row, 1), :], (nrows, HEAD_DIM))


def _hgrn_head_offdiag(head, q, k, v, b, st_ref, b_s):
    blk = [slice(SUB * i, SUB * (i + 1)) for i in range(NSUB)]
    bend = [_bcast_row(b_s, head, SUB * i + SUB - 1, SUB) for i in range(NSUB)]
    qp = [q[blk[0]] * jnp.exp(b[blk[0]])]
    for i in range(1, NSUB):
        qp.append(q[blk[i]] * jnp.exp(b[blk[i]] - bend[i - 1]))
    kp = [k[blk[i]] * jnp.exp(bend[i] - b[blk[i]]) for i in range(NSUB)]

    st = st_ref[head]
    st_b = st.astype(BF16)
    qb = [qp[0]] + [qp[i] * jnp.exp(bend[i - 1]) for i in range(1, NSUB)]
    o_inter = _dot_nt(jnp.concatenate(qb, axis=0).astype(BF16), st_b)
    out = [o_inter[blk[i]] for i in range(NSUB)]

    v_b = v.astype(BF16)
    for i in range(1, NSUB):
        parts = [kp[j] * jnp.exp(bend[i - 1] - bend[j]) for j in range(i - 1)] + [kp[i - 1]]
        kcat = jnp.concatenate(parts, axis=0) if len(parts) > 1 else parts[0]
        a = _dot_nt(qp[i].astype(BF16), kcat.astype(BF16))
        out[i] = out[i] + _dot(a.astype(BF16), v_b[0:SUB * i])

    last = NSUB - 1
    kend = jnp.concatenate(
        [kp[j] * jnp.exp(bend[last] - bend[j]) for j in range(last)] + [kp[last]], axis=0)
    decay = jnp.exp(_bcast_row(b_s, head, CHUNK - 1, HEAD_DIM))
    st_ref[head] = st * decay + _dot_tn(v_b, kend.astype(BF16))
    return out


def _hgrn_pair_diag(heads, qs, bs, out, b_s, k_s, v_s, ones_bd):
    row = lax.broadcasted_iota(jnp.int32, (8, HEAD_DIM), 0)
    for i in range(NSUB):
        pieces = {h: [] for h in heads}
        for h, q, b in zip(heads, qs, bs):
            q_lo, q_hi = q[SUB * i:SUB * i + 8], q[SUB * i + 8:SUB * (i + 1)]
            b_lo, b_hi = b[SUB * i:SUB * i + 8], b[SUB * i + 8:SUB * (i + 1)]
            for s in range(SUB):
                r = SUB * i + s
                b_row = _bcast_row(b_s, h, r, 8)
                k_row = _bcast_row(k_s, h, r, 8)
                if s < 8:
                    d_lo = jnp.where(row >= s, b_lo - b_row, NEG_BIG)
                    pieces[h].append(q_lo * jnp.exp(d_lo) * k_row)
                    pieces[h].append(q_hi * jnp.exp(b_hi - b_row) * k_row)
                else:
                    d_hi = jnp.where(row >= s - 8, b_hi - b_row, NEG_BIG)
                    pieces[h].append(q_hi * jnp.exp(d_hi) * k_row)
        p = jnp.concatenate(
            [jnp.concatenate(pieces[h], axis=0) for h in heads], axis=1).astype(BF16)
        red = _dot(p, ones_bd)
        for hi, h in enumerate(heads):
            lanes = slice(HEAD_DIM * hi, HEAD_DIM * (hi + 1))
            acc_lo = out[h][i][0:8]
            acc_hi = out[h][i][8:SUB]
            n = 0
            for s in range(SUB):
                v_row = _bcast_row(v_s, h, SUB * i + s, 8)
                if s < 8:
                    acc_lo = acc_lo + red[8 * n:8 * n + 8, lanes] * v_row
                    n += 1
                acc_hi = acc_hi + red[8 * n:8 * n + 8, lanes] * v_row
                n += 1
            out[h][i] = jnp.concatenate([acc_lo, acc_hi], axis=0)


def _decay_consts(layer, lb_ref):
    if layer == 0:
        return None
    lb = _layer_lower_bound(lb_ref, layer)
    return jnp.log(lb), jnp.log1p(-lb), 1.0 - lb


def _gate_prep(consts, r0, z_s, q_s, k_s, g_s, b_s, tri):
    rows = slice(r0, r0 + SUB_TILE)
    q_abs = []
    for hd in range(HEADS):
        lanes = slice(HEAD_DIM * hd, HEAD_DIM * (hd + 1))
        fp = z_s[rows, HGRN_WIDTH + HEAD_DIM * hd:HGRN_WIDTH + HEAD_DIM * (hd + 1)]
        e = jnp.exp2(jnp.abs(fp) * NEG_LOG2E)
        log_sig = jnp.minimum(fp, 0.0) - jnp.log(1.0 + e)
        sig_neg = jnp.exp(log_sig - fp)
        if consts is None:
            log_f, kk = log_sig, sig_neg
        else:
            log_lb, log_1m, one_m = consts
            cc = log_1m[:, lanes] + log_sig
            la = log_lb[:, lanes]
            log_f = jnp.maximum(la, cc) + jnp.log(1.0 + jnp.exp2(jnp.abs(la - cc) * NEG_LOG2E))
            kk = one_m[:, lanes] * sig_neg
        k_s[rows, lanes] = kk
        b_s[rows, lanes] = log_f
        q = _silu(z_s[rows, lanes], HEAD_DIM ** -0.5)
        q_s[rows, lanes] = q
        q_abs.append(jnp.max(jnp.abs(q), axis=0, keepdims=True))
        g_s[rows, lanes] = _silu(
            z_s[rows, 3 * HGRN_WIDTH + HEAD_DIM * hd:3 * HGRN_WIDTH + HEAD_DIM * (hd + 1)])

    span = None
    for c in range(SUB_TILE // FCHUNK):
        crows = slice(r0 + FCHUNK * c, r0 + FCHUNK * (c + 1))
        log_f = b_s[crows, :]
        g1 = log_f.astype(BF16)
        r1 = log_f - g1.astype(F32)
        g2 = r1.astype(BF16)
        g3 = (r1 - g2.astype(F32)).astype(BF16)
        b = _dot(tri, g1) + _dot(tri, g2) + _dot(tri, g3)
        b_s[crows, :] = b
        mid = b[FCHUNK // 2 - 1:FCHUNK // 2]
        s = jnp.maximum(b[0:1] - mid, mid - b[FCHUNK - 1:FCHUNK])
        span = s if span is None else jnp.maximum(span, s)
    return span + jnp.maximum(jnp.log(jnp.concatenate(q_abs, axis=1)), 0.0)


def _hgrn_scores(c0, hd, z_s, q_s, k_s, b_s):
    lanes = slice(HEAD_DIM * hd, HEAD_DIM * (hd + 1))
    rows = slice(c0, c0 + FCHUNK)
    b = b_s[rows, lanes]
    mid = b_s[c0 + FCHUNK // 2 - 1:c0 + FCHUNK // 2, lanes]
    last = b_s[c0 + FCHUNK - 1:c0 + FCHUNK, lanes]
    v = z_s[rows, 2 * HGRN_WIDTH + HEAD_DIM * hd:2 * HGRN_WIDTH + HEAD_DIM * (hd + 1)]
    grow = jnp.exp(b - mid)
    qs = q_s[rows, lanes] * grow
    ks = k_s[rows, lanes] * (1.0 / grow)
    a = _dot(qs.astype(BF16), ks.T.astype(BF16))
    qe = (qs * jnp.exp(mid)).astype(BF16)
    ke = (ks * jnp.exp(last - mid)).astype(BF16)
    return a, qe, _dot(v.T.astype(BF16), ke), jnp.exp(last)


def _hgrn_outputs(c0, hd, st, scores, z_s, g_s, hg, mix_s, tril):
    a, qe, kv, decay = scores
    lanes = slice(HEAD_DIM * hd, HEAD_DIM * (hd + 1))
    rows = slice(c0, c0 + FCHUNK)
    v = z_s[rows, 2 * HGRN_WIDTH + HEAD_DIM * hd:2 * HGRN_WIDTH + HEAD_DIM * (hd + 1)]
    v = v.astype(BF16)
    a = jnp.where(tril, a, 0.0).astype(BF16)
    o = _dot(a, v) + _dot(qe, st.T.astype(BF16))
    st = st * decay + kv
    ms = jnp.mean(o * o, axis=-1, keepdims=True)
    on = o * lax.rsqrt(ms + EPS) * hg[:, lanes] * g_s[rows, lanes]
    mix_s[rows, lanes] = on.astype(BF16)
    return st


def _hgrn_any_range(z_s, q_s, k_s, g_s, b_s, hg, st_s, mix_s, cb_s, ck_s, cv_s):
    ones_bd = jnp.where(
        lax.broadcasted_iota(jnp.int32, (2 * HEAD_DIM, 2 * HEAD_DIM), 0) // HEAD_DIM
        == lax.broadcasted_iota(jnp.int32, (2 * HEAD_DIM, 2 * HEAD_DIM), 1) // HEAD_DIM,
        1.0, 0.0).astype(BF16)

    def chunk_body(c, carry):
        for half in range(FCHUNK // CHUNK):
            r0 = pl.multiple_of(c * FCHUNK + CHUNK * half, CHUNK)
            rows = pl.ds(r0, CHUNK)
            b = b_s[rows, :]
            if half:
                b = b - b_s[pl.ds(r0 - 1, 1), :]
            q = q_s[rows, :]
            kk = k_s[rows, :]
            v = z_s[rows, 2 * HGRN_WIDTH:3 * HGRN_WIDTH]
            gate = g_s[rows, :]
            cb_s[...] = jnp.stack([b[:, HEAD_DIM * hd:HEAD_DIM * (hd + 1)] for hd in range(HEADS)])
            ck_s[...] = jnp.stack([kk[:, HEAD_DIM * hd:HEAD_DIM * (hd + 1)] for hd in range(HEADS)])
            cv_s[...] = jnp.stack([v[:, HEAD_DIM * hd:HEAD_DIM * (hd + 1)] for hd in range(HEADS)])
            out = {}
            for hd in range(HEADS):
                lanes = slice(HEAD_DIM * hd, HEAD_DIM * (hd + 1))
                out[hd] = _hgrn_head_offdiag(hd, q[:, lanes], kk[:, lanes], v[:, lanes],
                                             b[:, lanes], st_s, cb_s)
            for pair in ((0, 1), (2, 3)):
                _hgrn_pair_diag(pair,
                                [q[:, HEAD_DIM * hd:HEAD_DIM * (hd + 1)] for hd in pair],
                                [b[:, HEAD_DIM * hd:HEAD_DIM * (hd + 1)] for hd in pair],
                                out, cb_s, ck_s, cv_s, ones_bd)
            for hd in range(HEADS):
                lanes = slice(HEAD_DIM * hd, HEAD_DIM * (hd + 1))
                oh = jnp.concatenate(out[hd], axis=0)
                ms = jnp.mean(oh * oh, axis=-1, keepdims=True)
                on = oh * lax.rsqrt(ms + EPS) * hg[:, lanes] * gate[:, lanes]
                mix_s[rows, lanes] = on.astype(BF16)
        return carry

    lax.fori_loop(0, TQ // FCHUNK, chunk_body, 0)


GATE_GROUPS = (1, 0, 3)
OTHER_GROUPS = (2, 5, 6, 4)


def _mixer_kernel(layer, x_ref, lb_ref, w_in_hbm, w_out_hbm, cw_ref, hg_ref, pre_g_ref,
                  post_g_ref, o_ref, w_in_ref, w_out_ref, w_stage, w_sem, h_s, z_s, q_s, k_s, g_s,
                  b_s, mix_s, cbuf, st_s, st_next_s, cb_s, ck_s, cv_s):
    t = pl.program_id(1)
    subs = [SUB_TILE * i for i in range(TQ // SUB_TILE)]

    @pl.when(_first_grid_step())
    def _():
        _load_weight_bf16(w_in_hbm, layer, w_in_ref, w_stage, w_sem, *w_stage.shape[1:])
        _load_weight_bf16(w_out_hbm, layer, w_out_ref, w_stage, w_sem, *w_stage.shape[1:])

    @pl.when(t == 0)
    def _():
        st_s[...] = jnp.zeros_like(st_s)
        cbuf[:, pl.ds(TQ, HALO), :] = jnp.zeros((CONV_WIDTH // LANES, HALO, LANES), F32)

    cbuf[:, 0:HALO, :] = cbuf[:, TQ:TQ + HALO, :]

    def in_project(r0):
        rows = slice(r0, r0 + SUB_TILE)
        h_s[rows, :] = _rmsnorm(x_ref[rows, :], pre_g_ref[...]).astype(BF16)
        for g in GATE_GROUPS + OTHER_GROUPS:
            cols = slice(512 * g, 512 * (g + 1))
            z_s[rows, cols] = _dot(h_s[rows, :], w_in_ref[:, cols])

    consts = _decay_consts(layer, lb_ref)
    tril = (lax.broadcasted_iota(jnp.int32, (FCHUNK, FCHUNK), 0)
            >= lax.broadcasted_iota(jnp.int32, (FCHUNK, FCHUNK), 1))
    tri = jnp.where(tril, 1.0, 0.0).astype(BF16)
    hg = hg_ref[...]
    cw = cw_ref[...]

    def out_project(r0):
        rows = slice(r0, r0 + SUB_TILE)
        mix = _dot(mix_s[rows, :], w_out_ref[...])
        o_ref[rows, :] = x_ref[rows, :] + _rmsnorm(mix, post_g_ref[...])

    state = [st_s[hd] for hd in range(HEADS)]
    span = None
    in_project(subs[0])
    for i, r0 in enumerate(subs):
        rows = slice(r0, r0 + SUB_TILE)
        s = _gate_prep(consts, r0, z_s, q_s, k_s, g_s, b_s, tri)
        span = s if span is None else jnp.maximum(span, s)
        chunk_rows = [r0 + FCHUNK * c for c in range(SUB_TILE // FCHUNK)]
        scores = {(hd, c0): _hgrn_scores(c0, hd, z_s, q_s, k_s, b_s)
                  for hd in range(HEADS) for c0 in chunk_rows}
        for hd in range(HEADS):
            for c0 in chunk_rows:
                state[hd] = _hgrn_outputs(c0, hd, state[hd], scores[hd, c0], z_s, g_s, hg,
                                          mix_s, tril)
        lo = HALO + r0
        for j in range(CONV_WIDTH // LANES):
            lanes = slice(LANES * j, LANES * (j + 1))
            cbuf[j, lo:lo + SUB_TILE, :] = (z_s[rows, 2560 + LANES * j:2560 + LANES * (j + 1)]
                                            * z_s[rows, 3072 + LANES * j:3072 + LANES * (j + 1)])
            y = (cw[2:3, lanes] * cbuf[j, lo:lo + SUB_TILE, :]
                 + cw[1:2, lanes] * cbuf[j, lo - 1:lo - 1 + SUB_TILE, :]
                 + cw[0:1, lanes] * cbuf[j, lo - 2:lo - 2 + SUB_TILE, :])
            gate_b = z_s[rows, 2048 + LANES * j:2048 + LANES * (j + 1)]
            mix_s[rows, HGRN_WIDTH + LANES * j:HGRN_WIDTH + LANES * (j + 1)] = (
                gate_b * y).astype(BF16)
        if i + 1 < len(subs):
            in_project(subs[i + 1])
        if i >= 1:
            out_project(subs[i - 1])
    out_project(subs[-1])
    for hd in range(HEADS):
        st_next_s[hd] = state[hd]

    @pl.when(jnp.logical_not(jnp.max(span) <= MAX_LOG_SPAN))
    def _():
        st_next_s[...] = st_s[...]
        _hgrn_any_range(z_s, q_s, k_s, g_s, b_s, hg, st_next_s, mix_s, cb_s, ck_s, cv_s)
        for r0 in subs:
            out_project(r0)

    st_s[...] = st_next_s[...]


def _ffn_kernel(layer, x_ref, w_up_hbm, cw_ref, w_down_hbm, pre_g_ref, post_g_ref, o_ref,
                w_up_ref, w_down_ref, up_stage, down_stage, w_sem, ubuf):
    t = pl.program_id(1)

    @pl.when(_first_grid_step())
    def _():
        _load_weight_bf16(w_up_hbm, layer, w_up_ref, up_stage, w_sem, *up_stage.shape[1:])
        _load_weight_bf16(w_down_hbm, layer, w_down_ref, down_stage, w_sem, *down_stage.shape[1:])

    @pl.when(t == 0)
    def _():
        ubuf[:, pl.ds(TQ_FFN, HALO), :] = jnp.zeros((2 * D_FF // LANES, HALO, LANES), F32)

    ubuf[:, 0:HALO, :] = ubuf[:, TQ_FFN:TQ_FFN + HALO, :]
    per_block = FF_BLOCK // LANES
    n_blocks = D_FF // FF_BLOCK

    for sub in range(TQ_FFN // SUB_TILE):
        r0 = SUB_TILE * sub
        x = x_ref[r0:r0 + SUB_TILE, :]
        h = _rmsnorm(x, pre_g_ref[...]).astype(BF16)

        def conv(slab):
            w = cw_ref[:, LANES * slab:LANES * (slab + 1)]
            lo = HALO + r0
            return (w[2:3] * ubuf[slab, lo:lo + SUB_TILE, :]
                    + w[1:2] * ubuf[slab, lo - 1:lo - 1 + SUB_TILE, :]
                    + w[0:1] * ubuf[slab, lo - 2:lo - 2 + SUB_TILE, :])

        def up_project(g):
            for base in (FF_BLOCK * g, D_FF + FF_BLOCK * g):
                u = _dot(h, w_up_ref[:, base:base + FF_BLOCK])
                for j in range(per_block):
                    ubuf[base // LANES + j, HALO + r0:HALO + r0 + SUB_TILE, :] = (
                        u[:, LANES * j:LANES * (j + 1)])

        def activate(g):
            act = []
            for j in range(per_block):
                gate = conv(FF_BLOCK * g // LANES + j)
                val = conv((D_FF + FF_BLOCK * g) // LANES + j)
                act.append((gate * _sigmoid(gate) * val).astype(BF16))
            return jnp.concatenate(act, axis=1)

        def down_project(y, g, act):
            part = _dot(act, w_down_ref[FF_BLOCK * g:FF_BLOCK * (g + 1), :])
            return part if y is None else y + part

        y = None
        prev_act = None
        up_project(0)
        for g in range(n_blocks):
            if g + 1 < n_blocks:
                up_project(g + 1)
            if g >= 1:
                y = down_project(y, g - 1, prev_act)
            prev_act = activate(g)
        y = down_project(y, n_blocks - 1, prev_act)

        o_ref[r0:r0 + SUB_TILE, :] = x + _rmsnorm(y, post_g_ref[...])


def _x_spec(tq):
    return pl.BlockSpec((None, tq, D_MODEL), lambda b, t: (b, t, 0))


def _layer_spec(shape, layer):
    return pl.BlockSpec((None,) + shape, lambda b, t: (layer,) + (0,) * len(shape),
                        pipeline_mode=pl.Buffered(1))


_HBM_SPEC = pl.BlockSpec(memory_space=pl.ANY)


_COMPILER_PARAMS = pltpu.CompilerParams(
    dimension_semantics=("arbitrary", "arbitrary"), vmem_limit_bytes=VMEM_LIMIT_BYTES)


def _mixer_call(layer, x, lb_param, w_in, w_out, conv_w, hgrn_g, pre_g, post_g):
    batch, seq, _ = x.shape
    return pl.pallas_call(
        functools.partial(_mixer_kernel, layer),
        out_shape=jax.ShapeDtypeStruct(x.shape, x.dtype),
        grid=(batch, seq // TQ),
        in_specs=[
            _x_spec(TQ),
            pl.BlockSpec((DEPTH, HGRN_WIDTH), lambda b, t: (0, 0)),
            _HBM_SPEC,
            _HBM_SPEC,
            _layer_spec((3, CONV_WIDTH), layer),
            _layer_spec((1, HGRN_WIDTH), layer),
            _layer_spec((1, D_MODEL), layer),
            _layer_spec((1, D_MODEL), layer),
        ],
        out_specs=_x_spec(TQ),
        scratch_shapes=[
            pltpu.VMEM((D_MODEL, IN_COLS), BF16),
            pltpu.VMEM((D_MODEL, D_MODEL), BF16),
            pltpu.VMEM((W_SLOTS,) + W_STAGE, F32),
            pltpu.SemaphoreType.DMA((W_SLOTS,)),
            pltpu.VMEM((TQ, D_MODEL), BF16),
            pltpu.VMEM((TQ, IN_COLS), F32),
            pltpu.VMEM((TQ, HGRN_WIDTH), F32),
            pltpu.VMEM((TQ, HGRN_WIDTH), F32),
            pltpu.VMEM((TQ, HGRN_WIDTH), F32),
            pltpu.VMEM((TQ, HGRN_WIDTH), F32),
            pltpu.VMEM((TQ, D_MODEL), BF16),
            pltpu.VMEM((CONV_WIDTH // LANES, TQ + HALO, LANES), F32),
            pltpu.VMEM((HEADS, HEAD_DIM, HEAD_DIM), F32),
            pltpu.VMEM((HEADS, HEAD_DIM, HEAD_DIM), F32),
            pltpu.VMEM((HEADS, CHUNK, HEAD_DIM), F32),
            pltpu.VMEM((HEADS, CHUNK, HEAD_DIM), F32),
            pltpu.VMEM((HEADS, CHUNK, HEAD_DIM), F32),
        ],
        compiler_params=_COMPILER_PARAMS,
        name=f"mixer_l{layer}",
    )(x, lb_param, w_in, w_out, conv_w, hgrn_g, pre_g, post_g)


def _ffn_call(layer, x, w_up, conv_w, w_down, pre_g, post_g):
    batch, seq, _ = x.shape
    return pl.pallas_call(
        functools.partial(_ffn_kernel, layer),
        out_shape=jax.ShapeDtypeStruct(x.shape, x.dtype),
        grid=(batch, seq // TQ_FFN),
        in_specs=[
            _x_spec(TQ_FFN),
            _HBM_SPEC,
            _layer_spec((3, 2 * D_FF), layer),
            _HBM_SPEC,
            _layer_spec((1, D_MODEL), layer),
            _layer_spec((1, D_MODEL), layer),
        ],
        out_specs=_x_spec(TQ_FFN),
        scratch_shapes=[
            pltpu.VMEM((D_MODEL, 2 * D_FF), BF16),
            pltpu.VMEM((D_FF, D_MODEL), BF16),
            pltpu.VMEM((W_SLOTS,) + W_STAGE, F32),
            pltpu.VMEM((W_SLOTS,) + W_STAGE_DOWN, F32),
            pltpu.SemaphoreType.DMA((W_SLOTS,)),
            pltpu.VMEM((2 * D_FF // LANES, TQ_FFN + HALO, LANES), F32),
        ],
        compiler_params=_COMPILER_PARAMS,
        name=f"ffn_l{layer}",
    )(x, w_up, conv_w, w_down, pre_g, post_g)


def kernel(x, lb_param, w_in, w_out, conv_w, ffn_w_up, ffn_conv_w, ffn_w_down, hgrn_norm_g,
           pre_mix_g, post_mix_g, pre_ffn_g, post_ffn_g):
    assert x.shape[1] % TQ == 0 and x.shape[1] % TQ_FFN == 0 and x.shape[2] == D_MODEL
    hgrn_g = hgrn_norm_g[:, None, :]
    pre_mix, post_mix = pre_mix_g[:, None, :], post_mix_g[:, None, :]
    pre_ffn, post_ffn = pre_ffn_g[:, None, :], post_ffn_g[:, None, :]
    for layer in range(DEPTH):
        x = _mixer_call(layer, x, lb_param, w_in, w_out, conv_w, hgrn_g, pre_mix, post_mix)
        x = _ffn_call(layer, x, ffn_w_up, ffn_conv_w, ffn_w_down, pre_ffn, post_ffn)
    return x
```
